```python
import math
import jax, jax.numpy as jnp
from jax import lax
import numpy as np

D_MODEL = 1024
BATCH = 8
SEQ = 2048
DEPTH = 1
DEC_BATCH = 128
DEC_SEQ = 8
PAST_LEN = 16384
PAGE_SIZE = 128

D_MIX = D_MODEL
POOL_WIDTH = D_MIX // 2
POOL_WINDOWS = (2, 4, 8, 16)
N_POOL_GROUPS = len(POOL_WINDOWS)
POOL_GROUP = POOL_WIDTH // N_POOL_GROUPS
POOL_BUF = max(POOL_WINDOWS) - 1
GLA_WIDTH = D_MIX - POOL_WIDTH
GLA_HEADS = 4
GLA_HEAD_V = GLA_WIDTH // GLA_HEADS
GLA_HEAD_K = GLA_HEAD_V // 2
GLA_KEY = GLA_HEADS * GLA_HEAD_K
GLA_GATE_RANK = 16
GLA_GATE_NORM = 16.0
GLA_CHUNK = 64
IN_SIZES = (POOL_WIDTH, GLA_KEY, GLA_KEY, GLA_WIDTH, GLA_GATE_RANK, GLA_WIDTH)
D_IN = sum(IN_SIZES)
D_FF = 2816
N_ADA = 9
EPS = 1e-6

kernel_name = "hymba_pool_gla_macaron_adaln_step"


def rms_norm(x, g):
    x32 = x.astype(jnp.float32)
    y = x32 * lax.rsqrt(jnp.mean(x32 * x32, axis=-1, keepdims=True) + EPS)
    return (y * g.astype(jnp.float32)).astype(x.dtype)


def modulate(h, shift, scale):
    return h * (1 + scale) + shift


def swiglu(h, w_up, w_down):
    g, u = jnp.split(h @ w_up, 2, axis=-1)
    return (jax.nn.silu(g) * u) @ w_down


def split_in(proj):
    outs, off = [], 0
    for s in IN_SIZES:
        outs.append(proj[..., off:off + s])
        off += s
    return outs


def pool_mixer(u, buf, pos0, w_pool, pool_scale):
    B, T, P = u.shape
    ext = jnp.concatenate([buf.astype(u.dtype), u], axis=1).astype(jnp.float32)
    cs = jnp.concatenate([jnp.zeros((B, 1, P), jnp.float32), jnp.cumsum(ext, axis=1)], axis=1)
    pos = pos0 + jnp.arange(T, dtype=jnp.int32)
    cur = ext[:, POOL_BUF:, :]
    outs = []
    for g, w in enumerate(POOL_WINDOWS):
        lo, hi = g * POOL_GROUP, (g + 1) * POOL_GROUP
        win = cs[:, POOL_BUF + 1:POOL_BUF + 1 + T, lo:hi] - cs[:, POOL_BUF + 1 - w:POOL_BUF + 1 - w + T, lo:hi]
        cnt = jnp.minimum(pos + 1, w).astype(jnp.float32)[None, :, None]
        p = win / cnt - cur[:, :, lo:hi]
        outs.append(p @ w_pool[g].astype(jnp.float32))
    z = jnp.concatenate(outs, axis=-1) * pool_scale.astype(jnp.float32)
    new_buf = ext[:, -POOL_BUF:, :]
    return z.astype(u.dtype), new_buf


def gla_mixer(q, k, v, log_a, S0):
    B, T, H, dk = q.shape
    C = math.gcd(T, GLA_CHUNK)
    N = T // C

    def blk(t):
        return t.reshape(B, N, C, H, t.shape[-1]).transpose(0, 3, 1, 2, 4)

    q, k, v, log_a = blk(q), blk(k), blk(v), blk(log_a)
    b = jnp.cumsum(log_a, axis=3)
    b_last = b[:, :, :, -1:, :]
    q_in = q * jnp.exp(b) * (dk ** -0.5)
    k_in = k * jnp.exp(-b)
    k_dec = k * jnp.exp(b_last - b)
    mask = jnp.tril(jnp.ones((C, C), dtype=bool))
    att = jnp.where(mask, jnp.einsum('bhncd,bhnsd->bhncs', q_in, k_in), 0.0)
    o = jnp.einsum('bhncs,bhnse->bhnce', att, v)
    upd = jnp.einsum('bhnsd,bhnse->bhnde', k_dec, v)
    decay = jnp.exp(b_last[:, :, :, 0, :])

    def step(S, xs):
        dec, u = xs
        return dec[..., None] * S + u, S

    S_fin, S_prev = lax.scan(step, S0, (jnp.moveaxis(decay, 2, 0), jnp.moveaxis(upd, 2, 0)))
    S_prev = jnp.moveaxis(S_prev, 0, 2)
    o = o + jnp.einsum('bhncd,bhnde->bhnce', q_in, S_prev)
    o = o.transpose(0, 2, 3, 1, 4).reshape(B, T, H, v.shape[-1])
    return o, S_fin


def layer(x, c, pool_buf, gla_S, pos0, ln1, ln_mix, ln2, w_ada, b_ada, w_f1_up, w_f1_down,
          w_in, w_gk2, b_gk2, w_pool, pool_scale, gla_norm, w_out, w_f2_up, w_f2_down):
    B, T, _ = x.shape
    ada = jax.nn.silu(c) @ w_ada + b_ada
    sh1, sc1, g1, shm, scm, gm, sh2, sc2, g2 = [a[:, None, :] for a in jnp.split(ada, N_ADA, axis=-1)]
    h = modulate(rms_norm(x, ln1), sh1, sc1)
    x = x + 0.5 * g1 * swiglu(h, w_f1_up, w_f1_down)
    h = modulate(rms_norm(x, ln_mix), shm, scm)
    u, q, k, v, glr, og = split_in(h @ w_in)
    z_pool, new_buf = pool_mixer(u, pool_buf, pos0, w_pool, pool_scale)
    log_a = jax.nn.log_sigmoid((glr @ w_gk2 + b_gk2).astype(jnp.float32)) / GLA_GATE_NORM
    f32 = jnp.float32
    o, S_new = gla_mixer(q.reshape(B, T, GLA_HEADS, GLA_HEAD_K).astype(f32),
                         k.reshape(B, T, GLA_HEADS, GLA_HEAD_K).astype(f32),
                         v.reshape(B, T, GLA_HEADS, GLA_HEAD_V).astype(f32),
                         log_a.reshape(B, T, GLA_HEADS, GLA_HEAD_K),
                         gla_S.astype(f32))
    o = rms_norm(o, gla_norm).reshape(B, T, GLA_WIDTH) * jax.nn.silu(og.astype(f32))
    mix = jnp.concatenate([z_pool, o.astype(x.dtype)], axis=-1) @ w_out
    x = x + gm * mix
    h = modulate(rms_norm(x, ln2), sh2, sc2)
    x = x + 0.5 * g2 * swiglu(h, w_f2_up, w_f2_down)
    return x, new_buf.astype(x.dtype), S_new.astype(x.dtype)


def setup_inputs(seed: int = 0) -> dict:
    key = jax.random.key(seed)
    ks = jax.random.split(key, 24)

    def nrm(k, shape, scale=1.0):
        return jax.random.normal(k, shape, jnp.float32) * scale

    return {
        "x_prompt": nrm(ks[0], (BATCH, SEQ, D_MODEL)),
        "x_sample": nrm(ks[1], (DEC_BATCH, DEC_SEQ, D_MODEL)),
        "state_pool": nrm(ks[2], (DEPTH, DEC_BATCH, POOL_BUF, POOL_WIDTH)),
        "state_gla": nrm(ks[3], (DEPTH, DEC_BATCH, GLA_HEADS, GLA_HEAD_K, GLA_HEAD_V), 0.5),
        "c_prompt": nrm(ks[4], (BATCH, D_MODEL)),
        "c_sample": nrm(ks[5], (DEC_BATCH, D_MODEL)),
        "ln_ffn1": 1.0 + nrm(ks[6], (DEPTH, D_MODEL), 0.05),
        "ln_mix": 1.0 + nrm(ks[7], (DEPTH, D_MODEL), 0.05),
        "ln_ffn2": 1.0 + nrm(ks[8], (DEPTH, D_MODEL), 0.05),
        "w_ada": nrm(ks[9], (DEPTH, D_MODEL, N_ADA * D_MODEL), 0.5 * D_MODEL ** -0.5),
        "b_ada": nrm(ks[10], (DEPTH, N_ADA * D_MODEL), 0.01),
        "w_ffn1_up": nrm(ks[11], (DEPTH, D_MODEL, 2 * D_FF), D_MODEL ** -0.5),
        "w_ffn1_down": nrm(ks[12], (DEPTH, D_FF, D_MODEL), D_FF ** -0.5),
        "w_in": nrm(ks[13], (DEPTH, D_MODEL, D_IN), D_MODEL ** -0.5),
        "w_gk2": nrm(ks[14], (DEPTH, GLA_GATE_RANK, GLA_KEY), GLA_GATE_RANK ** -0.5),
        "b_gk2": nrm(ks[15], (DEPTH, GLA_KEY), 0.1),
        "w_pool": nrm(ks[16], (DEPTH, N_POOL_GROUPS, POOL_GROUP, POOL_GROUP), POOL_GROUP ** -0.5),
        "pool_scale": 1.0 + nrm(ks[17], (DEPTH, POOL_WIDTH), 0.1),
        "gla_norm": 1.0 + nrm(ks[18], (DEPTH, GLA_HEAD_V), 0.05),
        "w_out": nrm(ks[19], (DEPTH, D_MIX, D_MODEL), D_MIX ** -0.5),
        "w_ffn2_up": nrm(ks[20], (DEPTH, D_MODEL, 2 * D_FF), D_MODEL ** -0.5),
        "w_ffn2_down": nrm(ks[21], (DEPTH, D_FF, D_MODEL), D_FF ** -0.5),
        "ln_final": 1.0 + nrm(ks[22], (D_MODEL,), 0.05),
    }


def reference(x_prompt, x_sample, state_pool, state_gla, c_prompt, c_sample,
              ln_ffn1, ln_mix, ln_ffn2, w_ada, b_ada, w_ffn1_up, w_ffn1_down,
              w_in, w_gk2, b_gk2, w_pool, pool_scale, gla_norm, w_out,
              w_ffn2_up, w_ffn2_down, ln_final):
    bp = x_prompt.shape[0]
    xp, xs = x_prompt, x_sample
    pool_p, gla_p, pool_s, gla_s = [], [], [], []
    for l in range(DEPTH):
        w = (ln_ffn1[l], ln_mix[l], ln_ffn2[l], w_ada[l], b_ada[l], w_ffn1_up[l], w_ffn1_down[l],
             w_in[l], w_gk2[l], b_gk2[l], w_pool[l], pool_scale[l], gla_norm[l], w_out[l],
             w_ffn2_up[l], w_ffn2_down[l])
        buf0 = jnp.zeros((bp, POOL_BUF, POOL_WIDTH), xp.dtype)
        S0 = jnp.zeros((bp, GLA_HEADS, GLA_HEAD_K, GLA_HEAD_V), jnp.float32)
        xp, nb_p, ns_p = layer(xp, c_prompt, buf0, S0, 0, *w)
        xs, nb_s, ns_s = layer(xs, c_sample, state_pool[l], state_gla[l], PAST_LEN, *w)
        pool_p.append(nb_p); gla_p.append(ns_p); pool_s.append(nb_s); gla_s.append(ns_s)
    y_prompt = rms_norm(xp, ln_final)
    y_sample = rms_norm(xs, ln_final)
    new_pool_prompt = jnp.stack(pool_p)
    new_gla_prompt = jnp.stack(gla_p)
    new_pool_sample = jnp.stack(pool_s)
    new_gla_sample = jnp.stack(gla_s)
    return (y_prompt, y_sample, new_pool_prompt, new_gla_prompt, new_pool_sample, new_gla_sample)
```

```python
import functools

import jax
import jax.numpy as jnp
from jax import lax
from jax.experimental import pallas as pl
from jax.experimental.pallas import tpu as pltpu

D_MODEL = 1024
POOL_WIDTH = 512
POOL_WINDOWS = (2, 4, 8, 16)
POOL_GROUP = 128
POOL_BUF = 15
GLA_WIDTH = 512
GLA_HEADS = 4
GLA_HEAD_V = 128
GLA_HEAD_K = 64
GLA_KEY = 256
GLA_GATE_RANK = 16
GLA_GATE_NORM = 16.0
GLA_CHUNK = 64
D_FF = 2816
N_ADA = 9
EPS = 1e-6
PAST_LEN = 16384

LANES = 128
FFN_CHUNK = 256
TOKEN_TILE = 512
POOL_HEAD = 32
SAMPLE_SEG = 24
VMEM_LIMIT = 56 * 1024 * 1024

BF16 = jnp.bfloat16
F32 = jnp.float32

_CAT_U, _CAT_Q, _CAT_K, _CAT_V, _CAT_OG, _CAT_GLR = 0, 512, 768, 1024, 1536, 2048
_CAT_WIDTH = 2176


def _dot(a, b):
    return jnp.dot(a, b, preferred_element_type=F32)


def _const_spec(shape):
    nd = len(shape)
    return pl.BlockSpec(shape, lambda *_: (0,) * nd, pipeline_mode=pl.Buffered(1))


def _rms(x, g):
    return x * lax.rsqrt(jnp.mean(x * x, axis=-1, keepdims=True) + EPS) * g


def _log_sigmoid(x):
    return jnp.minimum(x, 0.0) - jnp.log1p(jnp.exp(-jnp.abs(x)))


def _modulation(ada_ref, mod_ref, rows, rows_per_seq):
    D = D_MODEL
    if rows_per_seq >= rows:
        ada = ada_ref[0]
        return ada[:, :D], ada[:, D:2 * D], ada[:, 2 * D:]
    for s in range(rows // rows_per_seq):
        mod_ref[s * rows_per_seq:(s + 1) * rows_per_seq, :] = jnp.broadcast_to(
            ada_ref[s:s + 1, :], (rows_per_seq, 3 * D))
    return mod_ref[:, :D], mod_ref[:, D:2 * D], mod_ref[:, 2 * D:]


def _ada_body(c_ref, w_ref, b_ref, o_ref):
    c = c_ref[...]
    a = (c * jax.nn.sigmoid(c)).astype(BF16)
    o_ref[...] = _dot(a, w_ref[...].astype(BF16)) + b_ref[...]


def _ada_call(c_all, w_ada, b_ada):
    n, d = c_all.shape
    nout = w_ada.shape[1]
    tn = 1024
    return pl.pallas_call(
        _ada_body,
        out_shape=jax.ShapeDtypeStruct((n, nout), F32),
        grid=(nout // tn,),
        in_specs=[pl.BlockSpec((n, d), lambda j: (0, 0)),
                  pl.BlockSpec((d, tn), lambda j: (0, j)),
                  pl.BlockSpec((1, tn), lambda j: (0, j))],
        out_specs=pl.BlockSpec((n, tn), lambda j: (0, j)),
        compiler_params=pltpu.CompilerParams(vmem_limit_bytes=VMEM_LIMIT),
        name="ada_proj",
    )(c_all, w_ada, b_ada.reshape(1, nout))


def _ffn_body(*refs, rows_per_seq, final):
    x_ref, ada_ref, ln_ref, wup_ref, wdn_ref = refs[:5]
    rest = refs[5:]
    if final:
        lnf_ref, rest = rest[0], rest[1:]
    o_ref, h_ref, a_ref = rest[:3]
    mod_ref = rest[3] if len(rest) > 3 else None
    rows = x_ref.shape[0]
    sh, sc, gt = _modulation(ada_ref, mod_ref, rows, rows_per_seq)

    h_ref[...] = (_rms(x_ref[...], ln_ref[...]) * (1.0 + sc) + sh).astype(BF16)
    for c in range(D_FF // FFN_CHUNK):
        lo = c * FFN_CHUNK
        g = _dot(h_ref[...], wup_ref[:, lo:lo + FFN_CHUNK])
        u = _dot(h_ref[...], wup_ref[:, D_FF + lo:D_FF + lo + FFN_CHUNK])
        a_ref[:, lo:lo + FFN_CHUNK] = (g * jax.nn.sigmoid(g) * u).astype(BF16)
    y = x_ref[...] + 0.5 * gt * _dot(a_ref[...], wdn_ref[...])
    if final:
        y = _rms(y, lnf_ref[...])
    o_ref[...] = y


def _ffn_call(x2d, ada, sub, ln, w_up, w_down, ln_final, *, rows_per_seq, name):
    n, d = x2d.shape
    tm = min(TOKEN_TILE, n)
    final = ln_final is not None
    if rows_per_seq >= tm:
        tiles_per_seq = rows_per_seq // tm
        ada_spec = pl.BlockSpec((1, 1, 3 * d), lambda i: (i // tiles_per_seq, 0, sub))
        scratch_mod = []
    else:
        g = tm // rows_per_seq
        ada_spec = pl.BlockSpec((g, 3 * d), lambda i: (i, sub))
        scratch_mod = [pltpu.VMEM((tm, 3 * d), F32)]
    in_specs = [pl.BlockSpec((tm, d), lambda i: (i, 0)), ada_spec, _const_spec((1, d)),
                _const_spec(w_up.shape), _const_spec(w_down.shape)]
    args = [x2d, ada, ln.reshape(1, d), w_up, w_down]
    if final:
        in_specs.append(_const_spec((1, d)))
        args.append(ln_final.reshape(1, d))
    return pl.pallas_call(
        functools.partial(_ffn_body, rows_per_seq=rows_per_seq, final=final),
        out_shape=jax.ShapeDtypeStruct((n, d), F32),
        grid=(n // tm,),
        in_specs=in_specs,
        out_specs=pl.BlockSpec((tm, d), lambda i: (i, 0)),
        scratch_shapes=[pltpu.VMEM((tm, d), BF16), pltpu.VMEM((tm, D_FF), BF16)] + scratch_mod,
        compiler_params=pltpu.CompilerParams(dimension_semantics=("arbitrary",),
                                             vmem_limit_bytes=VMEM_LIMIT),
        name=name,
    )(*args)


def _project_in(h_ref, wcat_ref, ext_dst, q_ref, k_ref, v_ref, og_ref, glr_ref):
    ext_dst[...] = _dot(h_ref[...], wcat_ref[:, _CAT_U:_CAT_Q])
    q_ref[...] = _dot(h_ref[...], wcat_ref[:, _CAT_Q:_CAT_K])
    k_ref[...] = _dot(h_ref[...], wcat_ref[:, _CAT_K:_CAT_V])
    v_ref[...] = _dot(h_ref[...], wcat_ref[:, _CAT_V:_CAT_OG])
    og_ref[...] = _dot(h_ref[...], wcat_ref[:, _CAT_OG:_CAT_GLR])
    glr_ref[...] = _dot(h_ref[...], wcat_ref[:, _CAT_GLR:_CAT_WIDTH])


def _window_sums(ext_ref, s1_ref, s2_ref, s4_ref, s8_ref):
    n = ext_ref.shape[0]
    g = POOL_GROUP
    s1_ref[8:n, :] = ext_ref[8:n, :] + ext_ref[7:n - 1, :]
    s2_ref[16:n, :] = s1_ref[16:n, g:4 * g] + s1_ref[14:n - 2, g:4 * g]
    s4_ref[24:n, :] = s2_ref[24:n, g:3 * g] + s2_ref[20:n - 4, g:3 * g]
    s8_ref[32:n, :] = s4_ref[32:n, g:2 * g] + s4_ref[24:n - 8, g:2 * g]


def _gate_decays(glr, wgk_ref, bgk_ref):
    gk = _dot(glr.astype(BF16), wgk_ref[...]) + bgk_ref[...]
    return _log_sigmoid(gk) * (1.0 / GLA_GATE_NORM)


def _head_stack(q_in, head_mask):
    return jnp.where(head_mask, jnp.concatenate([q_in] * GLA_HEADS, axis=0), 0.0).astype(BF16)


def _head_blocks(full, rows):
    return jnp.concatenate(
        [full[h * rows:(h + 1) * rows, h * GLA_HEAD_V:(h + 1) * GLA_HEAD_V] for h in range(GLA_HEADS)], axis=0)


def _gla_out(o_h, og_h, gnorm):
    o_n = o_h * lax.rsqrt(jnp.mean(o_h * o_h, axis=-1, keepdims=True) + EPS) * gnorm
    return (o_n * (og_h * jax.nn.sigmoid(og_h))).astype(BF16)


def _mix_prompt_body(x_ref, ada_ref, ln_ref, wcat_ref, wgk_ref, bgk_ref, wpool_ref, pscale_ref,
                     gnorm_ref, wout_ref, o_ref, pool_out_ref, gla_out_ref,
                     h_ref, ext_ref, s1_ref, s2_ref, s4_ref, s8_ref, q_ref, k_ref, v_ref, og_ref,
                     glr_ref, zo_ref, state_ref):
    j = pl.program_id(1)
    tt = x_ref.shape[0]
    n = POOL_HEAD + tt
    C = GLA_CHUNK

    @pl.when(j == 0)
    def _():
        ext_ref[0:POOL_HEAD, :] = jnp.zeros((POOL_HEAD, POOL_WIDTH), F32)
        state_ref[...] = jnp.zeros(state_ref.shape, F32)

    ada = ada_ref[0]
    sh, sc, gt = ada[:, :D_MODEL], ada[:, D_MODEL:2 * D_MODEL], ada[:, 2 * D_MODEL:]
    h_ref[...] = (_rms(x_ref[...], ln_ref[...]) * (1.0 + sc) + sh).astype(BF16)
    _project_in(h_ref, wcat_ref, ext_ref.at[POOL_HEAD:n, :], q_ref, k_ref, v_ref, og_ref, glr_ref)

    _window_sums(ext_ref, s1_ref, s2_ref, s4_ref, s8_ref)
    pos1 = lax.broadcasted_iota(jnp.int32, (tt, POOL_GROUP), 0) + (j * tt + 1)
    wins = (s1_ref, s2_ref, s4_ref, s8_ref)
    for g, w in enumerate(POOL_WINDOWS):
        lanes = slice(g * POOL_GROUP, (g + 1) * POOL_GROUP)
        cnt = jnp.minimum(pos1, w).astype(F32)
        p = wins[g][POOL_HEAD:n, 0:POOL_GROUP] / cnt - ext_ref[POOL_HEAD:n, lanes]
        z = _dot(p.astype(BF16), wpool_ref[g]) * pscale_ref[:, lanes]
        zo_ref[:, lanes] = z.astype(BF16)

    row = lax.broadcasted_iota(jnp.int32, (GLA_HEADS * C, GLA_KEY), 0)
    lane = lax.broadcasted_iota(jnp.int32, (GLA_HEADS * C, GLA_KEY), 1)
    head_mask = (row // C) == (lane // GLA_HEAD_K)
    arow = lax.broadcasted_iota(jnp.int32, (GLA_HEADS * C, C), 0)
    acol = lax.broadcasted_iota(jnp.int32, (GLA_HEADS * C, C), 1)
    causal = (arow % C) >= acol
    tril = (lax.broadcasted_iota(jnp.int32, (C, C), 0) >= lax.broadcasted_iota(jnp.int32, (C, C), 1)).astype(F32)
    gnorm = gnorm_ref[...]
    for c in range(tt // C):
        r = slice(c * C, (c + 1) * C)
        la = _gate_decays(glr_ref[r, :], wgk_ref, bgk_ref)
        b = jnp.dot(tril, la, precision=lax.Precision.HIGHEST, preferred_element_type=F32)
        b_last = b[C - 1:C, :]
        kk = k_ref[r, :]
        q_in = q_ref[r, :] * jnp.exp(b) * (GLA_HEAD_K ** -0.5)
        k_in = (kk * jnp.exp(-b)).astype(BF16)
        k_dec = (kk * jnp.exp(b_last - b)).astype(BF16)
        v_c = v_ref[r, :].astype(BF16)
        qm = _head_stack(q_in, head_mask)
        att = lax.dot_general(qm, k_in, (((1,), (1,)), ((), ())), preferred_element_type=F32)
        att = jnp.where(causal, att, 0.0).astype(BF16)
        state = state_ref[...]
        o_inter = _dot(qm, state.astype(BF16))
        for h in range(GLA_HEADS):
            vl = slice(h * GLA_HEAD_V, (h + 1) * GLA_HEAD_V)
            o_h = _dot(att[h * C:(h + 1) * C, :], v_c[:, vl]) + o_inter[h * C:(h + 1) * C, :]
            zo_ref[r, POOL_WIDTH + h * GLA_HEAD_V:POOL_WIDTH + (h + 1) * GLA_HEAD_V] = _gla_out(
                o_h, og_ref[r, vl], gnorm)
        upd = _head_blocks(lax.dot_general(k_dec, v_c, (((0,), (0,)), ((), ())),
                                           preferred_element_type=F32), GLA_HEAD_K)
        decay = jnp.exp(b.T[:, C - 1:C])
        state_ref[...] = decay * state + upd

    o_ref[...] = x_ref[...] + gt * _dot(zo_ref[...], wout_ref[...])

    @pl.when(j == pl.num_programs(1) - 1)
    def _():
        pool_out_ref[0] = ext_ref[n - POOL_BUF:n, :]
        gla_out_ref[0] = state_ref[...]

    ext_ref[0:POOL_HEAD, :] = ext_ref[tt:n, :]


def _mix_prompt_call(x2d, ada_p, ln, wcat, wgk, bgk, wpool, pscale, gnorm, wout, *, batch, seq):
    d = D_MODEL
    tt = TOKEN_TILE
    nt = seq // tt
    n = POOL_HEAD + tt
    return pl.pallas_call(
        _mix_prompt_body,
        out_shape=(jax.ShapeDtypeStruct((batch * seq, d), F32),
                   jax.ShapeDtypeStruct((batch, POOL_BUF, POOL_WIDTH), F32),
                   jax.ShapeDtypeStruct((batch, GLA_KEY, GLA_HEAD_V), F32)),
        grid=(batch, nt),
        in_specs=[pl.BlockSpec((tt, d), lambda b, j: (b * nt + j, 0)),
                  pl.BlockSpec((1, 1, 3 * d), lambda b, j: (b, 0, 1)),
                  _const_spec((1, d)), _const_spec(wcat.shape), _const_spec(wgk.shape),
                  _const_spec((1, GLA_KEY)), _const_spec(wpool.shape), _const_spec((1, POOL_WIDTH)),
                  _const_spec((1, GLA_HEAD_V)), _const_spec(wout.shape)],
        out_specs=(pl.BlockSpec((tt, d), lambda b, j: (b * nt + j, 0)),
                   pl.BlockSpec((1, POOL_BUF, POOL_WIDTH), lambda b, j: (b, 0, 0)),
                   pl.BlockSpec((1, GLA_KEY, GLA_HEAD_V), lambda b, j: (b, 0, 0))),
        scratch_shapes=[pltpu.VMEM((tt, d), BF16),
                        pltpu.VMEM((n, POOL_WIDTH), F32), pltpu.VMEM((n, 4 * POOL_GROUP), F32),
                        pltpu.VMEM((n, 3 * POOL_GROUP), F32), pltpu.VMEM((n, 2 * POOL_GROUP), F32),
                        pltpu.VMEM((n, POOL_GROUP), F32),
                        pltpu.VMEM((tt, GLA_KEY), F32), pltpu.VMEM((tt, GLA_KEY), F32),
                        pltpu.VMEM((tt, GLA_WIDTH), F32), pltpu.VMEM((tt, GLA_WIDTH), F32),
                        pltpu.VMEM((tt, LANES), F32), pltpu.VMEM((tt, d), BF16),
                        pltpu.VMEM((GLA_KEY, GLA_HEAD_V), F32)],
        compiler_params=pltpu.CompilerParams(dimension_semantics=("arbitrary", "arbitrary"),
                                             vmem_limit_bytes=VMEM_LIMIT),
        name="mix_prompt",
    )(x2d, ada_p, ln.reshape(1, d), wcat, wgk, bgk.reshape(1, GLA_KEY), wpool,
      pscale.reshape(1, POOL_WIDTH), gnorm.reshape(1, GLA_HEAD_V), wout)


def _mix_sample_body(x_ref, ada_ref, pool_ref, gla_ref, ln_ref, wcat_ref, wgk_ref, bgk_ref, wpool_ref,
                     pscale_ref, gnorm_ref, wout_ref, o_ref, pool_out_ref, gla_out_ref,
                     mod_ref, h_ref, u_ref, ext_ref, s1_ref, s2_ref, s4_ref, s8_ref, p_ref,
                     q_ref, k_ref, v_ref, og_ref, glr_ref, zo_ref, oint_ref, *, steps, pos0):
    rows = x_ref.shape[0]
    T = steps
    G = rows // T
    C = GLA_CHUNK
    SEG = SAMPLE_SEG

    sh, sc, gt = _modulation(ada_ref, mod_ref, rows, T)
    h_ref[...] = (_rms(x_ref[...], ln_ref[...]) * (1.0 + sc) + sh).astype(BF16)
    _project_in(h_ref, wcat_ref, u_ref, q_ref, k_ref, v_ref, og_ref, glr_ref)

    ext_ref[0:POOL_HEAD, :] = jnp.zeros((POOL_HEAD, POOL_WIDTH), F32)
    for s in range(G):
        base = POOL_HEAD + s * SEG
        ext_ref[base:base + 16, :] = pool_ref[s]
        ext_ref[base + 16:base + SEG, :] = u_ref[s * T:(s + 1) * T, :]
    _window_sums(ext_ref, s1_ref, s2_ref, s4_ref, s8_ref)
    pos1 = lax.broadcasted_iota(jnp.int32, (T, POOL_GROUP), 0) + (pos0 + 1)
    wins = (s1_ref, s2_ref, s4_ref, s8_ref)
    for s in range(G):
        base = POOL_HEAD + s * SEG
        for g, w in enumerate(POOL_WINDOWS):
            lanes = slice(g * POOL_GROUP, (g + 1) * POOL_GROUP)
            cnt = jnp.minimum(pos1, w).astype(F32)
            p_ref[s * T:(s + 1) * T, lanes] = (wins[g][base + 16:base + SEG, 0:POOL_GROUP] / cnt
                                               - u_ref[s * T:(s + 1) * T, lanes])
        pool_out_ref[s] = ext_ref[base + SEG - POOL_BUF:base + SEG, :]
    for g in range(len(POOL_WINDOWS)):
        lanes = slice(g * POOL_GROUP, (g + 1) * POOL_GROUP)
        z = _dot(p_ref[:, lanes].astype(BF16), wpool_ref[g]) * pscale_ref[:, lanes]
        zo_ref[:, lanes] = z.astype(BF16)

    la = _gate_decays(glr_ref[...], wgk_ref, bgk_ref)
    step = lax.broadcasted_iota(jnp.int32, (rows, GLA_KEY), 0) % T
    b = la
    shift = 1
    while shift < T:
        b = b + jnp.where(step >= shift, pltpu.roll(b, shift, axis=0), 0.0)
        shift *= 2
    b_last = jnp.broadcast_to(b.reshape(G, T, GLA_KEY)[:, T - 1:T, :], (G, T, GLA_KEY)).reshape(rows, GLA_KEY)
    kk = k_ref[...]
    q_in = q_ref[...] * jnp.exp(b) * (GLA_HEAD_K ** -0.5)
    k_in = (kk * jnp.exp(-b)).astype(BF16)
    k_dec = (kk * jnp.exp(b_last - b)).astype(BF16)
    v_all = v_ref[...].astype(BF16)
    decay_t = jnp.exp(b_last).T

    hrow = lax.broadcasted_iota(jnp.int32, (GLA_HEADS * T, GLA_KEY), 0)
    hlane = lax.broadcasted_iota(jnp.int32, (GLA_HEADS * T, GLA_KEY), 1)
    seq_head_mask = (hrow // T) == (hlane // GLA_HEAD_K)
    for s in range(G):
        r = slice(s * T, (s + 1) * T)
        state = gla_ref[s]
        qm = _head_stack(q_in[r, :], seq_head_mask)
        o_int = _dot(qm, state.astype(BF16))
        for h in range(GLA_HEADS):
            oint_ref[h, r, :] = o_int[h * T:(h + 1) * T, :]
        upd = _head_blocks(lax.dot_general(k_dec[r, :], v_all[r, :], (((0,), (0,)), ((), ())),
                                           preferred_element_type=F32), GLA_HEAD_K)
        gla_out_ref[s] = decay_t[:, s * T:s * T + 1] * state + upd

    row = lax.broadcasted_iota(jnp.int32, (GLA_HEADS * C, GLA_KEY), 0)
    lane = lax.broadcasted_iota(jnp.int32, (GLA_HEADS * C, GLA_KEY), 1)
    head_mask = (row // C) == (lane // GLA_HEAD_K)
    arow = lax.broadcasted_iota(jnp.int32, (GLA_HEADS * C, C), 0) % C
    acol = lax.broadcasted_iota(jnp.int32, (GLA_HEADS * C, C), 1)
    causal = (arow >= acol) & ((arow // T) == (acol // T))
    gnorm = gnorm_ref[...]
    for c in range(rows // C):
        r = slice(c * C, (c + 1) * C)
        qm = _head_stack(q_in[r, :], head_mask)
        att = lax.dot_general(qm, k_in[r, :], (((1,), (1,)), ((), ())), preferred_element_type=F32)
        att = jnp.where(causal, att, 0.0).astype(BF16)
        for h in range(GLA_HEADS):
            vl = slice(h * GLA_HEAD_V, (h + 1) * GLA_HEAD_V)
            o_h = _dot(att[h * C:(h + 1) * C, :], v_all[r, vl]) + oint_ref[h, r, :]
            zo_ref[r, POOL_WIDTH + h * GLA_HEAD_V:POOL_WIDTH + (h + 1) * GLA_HEAD_V] = _gla_out(
                o_h, og_ref[r, vl], gnorm)

    o_ref[...] = x_ref[...] + gt * _dot(zo_ref[...], wout_ref[...])


def _mix_sample_call(x2d, ada_s, pool16, gla_state, ln, wcat, wgk, bgk, wpool, pscale, gnorm, wout, *,
                     steps, pos0):
    d = D_MODEL
    nseq = gla_state.shape[0]
    g = 32
    rows = g * steps
    n = POOL_HEAD + g * SAMPLE_SEG
    return pl.pallas_call(
        functools.partial(_mix_sample_body, steps=steps, pos0=pos0),
        out_shape=(jax.ShapeDtypeStruct((nseq * steps, d), F32),
                   jax.ShapeDtypeStruct((nseq, POOL_BUF, POOL_WIDTH), F32),
                   jax.ShapeDtypeStruct((nseq, GLA_KEY, GLA_HEAD_V), F32)),
        grid=(nseq // g,),
        in_specs=[pl.BlockSpec((rows, d), lambda i: (i, 0)),
                  pl.BlockSpec((g, 3 * d), lambda i: (i, 1)),
                  pl.BlockSpec((g, POOL_BUF + 1, POOL_WIDTH), lambda i: (i, 0, 0)),
                  pl.BlockSpec((g, GLA_KEY, GLA_HEAD_V), lambda i: (i, 0, 0)),
                  _const_spec((1, d)), _const_spec(wcat.shape), _const_spec(wgk.shape),
                  _const_spec((1, GLA_KEY)), _const_spec(wpool.shape), _const_spec((1, POOL_WIDTH)),
                  _const_spec((1, GLA_HEAD_V)), _const_spec(wout.shape)],
        out_specs=(pl.BlockSpec((rows, d), lambda i: (i, 0)),
                   pl.BlockSpec((g, POOL_BUF, POOL_WIDTH), lambda i: (i, 0, 0)),
                   pl.BlockSpec((g, GLA_KEY, GLA_HEAD_V), lambda i: (i, 0, 0))),
        scratch_shapes=[pltpu.VMEM((rows, 3 * d), F32), pltpu.VMEM((rows, d), BF16),
                        pltpu.VMEM((rows, POOL_WIDTH), F32),
                        pltpu.VMEM((n, POOL_WIDTH), F32), pltpu.VMEM((n, 4 * POOL_GROUP), F32),
                        pltpu.VMEM((n, 3 * POOL_GROUP), F32), pltpu.VMEM((n, 2 * POOL_GROUP), F32),
                        pltpu.VMEM((n, POOL_GROUP), F32), pltpu.VMEM((rows, POOL_WIDTH), F32),
                        pltpu.VMEM((rows, GLA_KEY), F32), pltpu.VMEM((rows, GLA_KEY), F32),
                        pltpu.VMEM((rows, GLA_WIDTH), F32), pltpu.VMEM((rows, GLA_WIDTH), F32),
                        pltpu.VMEM((rows, LANES), F32), pltpu.VMEM((rows, d), BF16),
                        pltpu.VMEM((GLA_HEADS, rows, GLA_HEAD_V), F32)],
        compiler_params=pltpu.CompilerParams(dimension_semantics=("arbitrary",),
                                             vmem_limit_bytes=VMEM_LIMIT),
        name="mix_sample",
    )(x2d, ada_s, pool16, gla_state, ln.reshape(1, d), wcat, wgk, bgk.reshape(1, GLA_KEY), wpool,
      pscale.reshape(1, POOL_WIDTH), gnorm.reshape(1, GLA_HEAD_V), wout)


def _cat_in_weights(w_in):
    u, q, k, v, glr, og = jnp.split(w_in, (512, 768, 1024, 1536, 1552), axis=1)
    glr = jnp.pad(glr, ((0, 0), (0, LANES - GLA_GATE_RANK)))
    return jnp.concatenate([u, q, k, v, og, glr], axis=1).astype(BF16)


def kernel(x_prompt, x_sample, state_pool, state_gla, c_prompt, c_sample, ln_ffn1, ln_mix, ln_ffn2,
           w_ada, b_ada, w_ffn1_up, w_ffn1_down, w_in, w_gk2, b_gk2, w_pool, pool_scale, gla_norm,
           w_out, w_ffn2_up, w_ffn2_down, ln_final):
    bp, seq, d = x_prompt.shape
    bs, steps, _ = x_sample.shape
    depth = ln_ffn1.shape[0]
    xp = x_prompt.reshape(bp * seq, d)
    xs = x_sample.reshape(bs * steps, d)
    c_all = jnp.concatenate([c_prompt, c_sample], axis=0)
    pool_p, gla_p, pool_s, gla_s = [], [], [], []
    for l in range(depth):
        ada = _ada_call(c_all, w_ada[l], b_ada[l])
        ada_p = ada[:bp].reshape(bp, 1, N_ADA * d)
        ada_s = ada[bp:]
        w1u, w1d = w_ffn1_up[l].astype(BF16), w_ffn1_down[l].astype(BF16)
        w2u, w2d = w_ffn2_up[l].astype(BF16), w_ffn2_down[l].astype(BF16)
        wcat = _cat_in_weights(w_in[l])
        wgk = jnp.pad(w_gk2[l], ((0, LANES - GLA_GATE_RANK), (0, 0))).astype(BF16)
        wpool = w_pool[l].astype(BF16)
        wout = w_out[l].astype(BF16)
        lnf = ln_final if l == depth - 1 else None

        xp = _ffn_call(xp, ada_p, 0, ln_ffn1[l], w1u, w1d, None, rows_per_seq=seq, name="ffn1_prompt")
        xs = _ffn_call(xs, ada_s, 0, ln_ffn1[l], w1u, w1d, None, rows_per_seq=steps, name="ffn1_sample")

        xp, nb_p, ns_p = _mix_prompt_call(xp, ada_p, ln_mix[l], wcat, wgk, b_gk2[l], wpool, pool_scale[l],
                                          gla_norm[l], wout, batch=bp, seq=seq)
        pool16 = jnp.pad(state_pool[l], ((0, 0), (1, 0), (0, 0)))
        xs, nb_s, ns_s = _mix_sample_call(xs, ada_s, pool16, state_gla[l].reshape(bs, GLA_KEY, GLA_HEAD_V),
                                          ln_mix[l], wcat, wgk, b_gk2[l], wpool, pool_scale[l], gla_norm[l],
                                          wout, steps=steps, pos0=PAST_LEN)

        xp = _ffn_call(xp, ada_p, 2, ln_ffn2[l], w2u, w2d, lnf, rows_per_seq=seq, name="ffn2_prompt")
        xs = _ffn_call(xs, ada_s, 2, ln_ffn2[l], w2u, w2d, lnf, rows_per_seq=steps, name="ffn2_sample")
        pool_p.append(nb_p)
        gla_p.append(ns_p.reshape(bp, GLA_HEADS, GLA_HEAD_K, GLA_HEAD_V))
        pool_s.append(nb_s)
        gla_s.append(ns_s.reshape(bs, GLA_HEADS, GLA_HEAD_K, GLA_HEAD_V))
    return (xp.reshape(bp, seq, d), xs.reshape(bs, steps, d), jnp.stack(pool_p), jnp.stack(gla_p),
            jnp.stack(pool_s), jnp.stack(gla_s))
```

```python
import functools

import jax
import jax.numpy as jnp
from jax import lax
from jax.experimental import pallas as pl
from jax.experimental.pallas import tpu as pltpu

D_MODEL = 1024
POOL_WIDTH = 512
POOL_WINDOWS = (2, 4, 8, 16)
POOL_GROUP = 128
POOL_BUF = 15
GLA_WIDTH = 512
GLA_HEADS = 4
GLA_HEAD_V = 128
GLA_HEAD_K = 64
GLA_KEY = 256
GLA_GATE_RANK = 16
GLA_GATE_NORM = 16.0
GLA_CHUNK = 64
D_FF = 2816
N_ADA = 9
EPS = 1e-6
PAST_LEN = 16384

LANES = 128
FFN_CHUNK = 256
TOKEN_TILE = 512
POOL_HEAD = 32
SAMPLE_SEG = 24
VMEM_LIMIT = 56 * 1024 * 1024

BF16 = jnp.bfloat16
F32 = jnp.float32

_CAT_U, _CAT_Q, _CAT_K, _CAT_V, _CAT_OG, _CAT_GLR = 0, 512, 768, 1024, 1536, 2048
_CAT_WIDTH = 2176


def _dot(a, b):
    return jnp.dot(a, b, preferred_element_type=F32)


def _const_spec(shape):
    nd = len(shape)
    return pl.BlockSpec(shape, lambda *_: (0,) * nd, pipeline_mode=pl.Buffered(1))


def _rms(x, g):
    return x * lax.rsqrt(jnp.mean(x * x, axis=-1, keepdims=True) + EPS) * g


def _log_sigmoid(x):
    return jnp.minimum(x, 0.0) - jnp.log1p(jnp.exp(-jnp.abs(x)))


def _modulation(ada_ref, mod_ref, rows, rows_per_seq):
    D = D_MODEL
    if rows_per_seq >= rows:
        ada = ada_ref[0]
        return ada[:, :D], ada[:, D:2 * D], ada[:, 2 * D:]
    for s in range(rows // rows_per_seq):
        mod_ref[s * rows_per_seq:(s + 1) * rows_per_seq, :] = jnp.broadcast_to(
            ada_ref[s:s + 1, :], (rows_per_seq, 3 * D))
    return mod_ref[:, :D], mod_ref[:, D:2 * D], mod_ref[:, 2 * D:]


def _ada_body(c_ref, w_ref, b_ref, o_ref):
    c = c_ref[...]
    a = (c * jax.nn.sigmoid(c)).astype(BF16)
    o_ref[...] = _dot(a, w_ref[...].astype(BF16)) + b_ref[...]


def _ada_call(c_all, w_ada, b_ada):
    n, d = c_all.shape
    nout = w_ada.shape[1]
    tn = 1024
    return pl.pallas_call(
        _ada_body,
        out_shape=jax.ShapeDtypeStruct((n, nout), F32),
        grid=(nout // tn,),
        in_specs=[pl.BlockSpec((n, d), lambda j: (0, 0)),
                  pl.BlockSpec((d, tn), lambda j: (0, j)),
                  pl.BlockSpec((1, tn), lambda j: (0, j))],
        out_specs=pl.BlockSpec((n, tn), lambda j: (0, j)),
        compiler_params=pltpu.CompilerParams(vmem_limit_bytes=VMEM_LIMIT),
        name="ada_proj",
    )(c_all, w_ada, b_ada.reshape(1, nout))


def _ffn_body(*refs, rows_per_seq, final):
    x_ref, ada_ref, ln_ref, wup_ref, wdn_ref = refs[:5]
    rest = refs[5:]
    if final:
        lnf_ref, rest = rest[0], rest[1:]
    o_ref, h_ref, a_ref = rest[:3]
    mod_ref = rest[3] if len(rest) > 3 else None
    rows = x_ref.shape[0]
    sh, sc, gt = _modulation(ada_ref, mod_ref, rows, rows_per_seq)

    h_ref[...] = (_rms(x_ref[...], ln_ref[...]) * (1.0 + sc) + sh).astype(BF16)
    for c in range(D_FF // FFN_CHUNK):
        lo = c * FFN_CHUNK
        g = _dot(h_ref[...], wup_ref[:, lo:lo + FFN_CHUNK])
        u = _dot(h_ref[...], wup_ref[:, D_FF + lo:D_FF + lo + FFN_CHUNK])
        a_ref[:, lo:lo + FFN_CHUNK] = (g * jax.nn.sigmoid(g) * u).astype(BF16)
    y = x_ref[...] + 0.5 * gt * _dot(a_ref[...], wdn_ref[...])
    if final:
        y = _rms(y, lnf_ref[...])
    o_ref[...] = y


def _ffn_call(x2d, ada, sub, ln, w_up, w_down, ln_final, *, rows_per_seq, name):
    n, d = x2d.shape
    tm = min(TOKEN_TILE, n)
    final = ln_final is not None
    if rows_per_seq >= tm:
        tiles_per_seq = rows_per_seq // tm
        ada_spec = pl.BlockSpec((1, 1, 3 * d), lambda i: (i // tiles_per_seq, 0, sub))
        scratch_mod = []
    else:
        g = tm // rows_per_seq
        ada_spec = pl.BlockSpec((g, 3 * d), lambda i: (i, sub))
        scratch_mod = [pltpu.VMEM((tm, 3 * d), F32)]
    in_specs = [pl.BlockSpec((tm, d), lambda i: (i, 0)), ada_spec, _const_spec((1, d)),
                _const_spec(w_up.shape), _const_spec(w_down.shape)]
    args = [x2d, ada, ln.reshape(1, d), w_up, w_down]
    if final:
        in_specs.append(_const_spec((1, d)))
        args.append(ln_final.reshape(1, d))
    return pl.pallas_call(
        functools.partial(_ffn_body, rows_per_seq=rows_per_seq, final=final),
        out_shape=jax.ShapeDtypeStruct((n, d), F32),
        grid=(n // tm,),
        in_specs=in_specs,
        out_specs=pl.BlockSpec((tm, d), lambda i: (i, 0)),
        scratch_shapes=[pltpu.VMEM((tm, d), BF16), pltpu.VMEM((tm, D_FF), BF16)] + scratch_mod,
        compiler_params=pltpu.CompilerParams(dimension_semantics=("arbitrary",),
                                             vmem_limit_bytes=VMEM_LIMIT),
        name=name,
    )(*args)


def _project_in(h_ref, wcat_ref, ext_dst, q_ref, k_ref, v_ref, og_ref):
    q_ref[...] = _dot(h_ref[...], wcat_ref[:, _CAT_Q:_CAT_K])
    k_ref[...] = _dot(h_ref[...], wcat_ref[:, _CAT_K:_CAT_V])
    v_ref[...] = _dot(h_ref[...], wcat_ref[:, _CAT_V:_CAT_OG])
    ext_dst[...] = _dot(h_ref[...], wcat_ref[:, _CAT_U:_CAT_Q])
    og_ref[...] = _dot(h_ref[...], wcat_ref[:, _CAT_OG:_CAT_GLR])


def _window_sums(ext_ref, s1_ref, s2_ref, s4_ref, s8_ref):
    n = ext_ref.shape[0]
    g = POOL_GROUP
    s1_ref[8:n, :] = ext_ref[8:n, :] + ext_ref[7:n - 1, :]
    s2_ref[16:n, :] = s1_ref[16:n, g:4 * g] + s1_ref[14:n - 2, g:4 * g]
    s4_ref[24:n, :] = s2_ref[24:n, g:3 * g] + s2_ref[20:n - 4, g:3 * g]
    s8_ref[32:n, :] = s4_ref[32:n, g:2 * g] + s4_ref[24:n - 8, g:2 * g]


def _gate_decays(h_ref, wcat_ref, wgk_ref, bgk_ref):
    glr = _dot(h_ref[...], wcat_ref[:, _CAT_GLR:_CAT_WIDTH])
    gk = _dot(glr.astype(BF16), wgk_ref[...]) + bgk_ref[...]
    return _log_sigmoid(gk) * (1.0 / GLA_GATE_NORM)


def _head_stack(q_in, head_mask):
    return jnp.where(head_mask, jnp.concatenate([q_in] * GLA_HEADS, axis=0), 0.0).astype(BF16)


def _head_blocks(full, rows):
    return jnp.concatenate(
        [full[h * rows:(h + 1) * rows, h * GLA_HEAD_V:(h + 1) * GLA_HEAD_V] for h in range(GLA_HEADS)], axis=0)


def _gla_out(o_h, og_h, gnorm):
    o_n = o_h * lax.rsqrt(jnp.mean(o_h * o_h, axis=-1, keepdims=True) + EPS) * gnorm
    return (o_n * (og_h * jax.nn.sigmoid(og_h))).astype(BF16)


def _mix_prompt_body(x_ref, ada_ref, ln_ref, wcat_ref, wgk_ref, bgk_ref, wpool_ref, pscale_ref,
                     gnorm_ref, wout_ref, o_ref, pool_out_ref, gla_out_ref,
                     h_ref, ext_ref, s1_ref, s2_ref, s4_ref, s8_ref, q_ref, k_ref, v_ref, og_ref,
                     att_ref, zo_ref, state_ref, bl_ref, qin_ref, kin_ref, kdec_ref, v16_ref, qm_ref,
                     go_ref, upd_ref):
    j = pl.program_id(1)
    tt = x_ref.shape[0]
    n = POOL_HEAD + tt
    C = GLA_CHUNK

    @pl.when(j == 0)
    def _():
        ext_ref[0:POOL_HEAD, :] = jnp.zeros((POOL_HEAD, POOL_WIDTH), F32)
        state_ref[...] = jnp.zeros(state_ref.shape, F32)

    ada = ada_ref[0]
    sh, sc, gt = ada[:, :D_MODEL], ada[:, D_MODEL:2 * D_MODEL], ada[:, 2 * D_MODEL:]
    h_ref[...] = (_rms(x_ref[...], ln_ref[...]) * (1.0 + sc) + sh).astype(BF16)
    la = _gate_decays(h_ref, wcat_ref, wgk_ref, bgk_ref)
    _project_in(h_ref, wcat_ref, ext_ref.at[POOL_HEAD:n, :], q_ref, k_ref, v_ref, og_ref)

    nc = tt // C
    step = lax.broadcasted_iota(jnp.int32, (tt, GLA_KEY), 0) % C
    b = la
    shift = 1
    while shift < C:
        b = b + jnp.where(step >= shift, pltpu.roll(b, shift, axis=0), 0.0)
        shift *= 2
    for c in range(nc):
        bl_ref[c:c + 1, :] = b[(c + 1) * C - 1:(c + 1) * C, :]
    b_last = jnp.broadcast_to(bl_ref[...][:, None, :], (nc, C, GLA_KEY)).reshape(tt, GLA_KEY)
    kk = k_ref[...]
    qin_ref[...] = (q_ref[...] * jnp.exp(b) * (GLA_HEAD_K ** -0.5)).astype(BF16)
    kin_ref[...] = (kk * jnp.exp(-b)).astype(BF16)
    kdec_ref[...] = (kk * jnp.exp(b_last - b)).astype(BF16)
    v16_ref[...] = v_ref[...].astype(BF16)
    decay_t = jnp.exp(bl_ref[...]).T

    _window_sums(ext_ref, s1_ref, s2_ref, s4_ref, s8_ref)
    pos1 = lax.broadcasted_iota(jnp.int32, (tt, POOL_GROUP), 0) + (j * tt + 1)
    wins = (s1_ref, s2_ref, s4_ref, s8_ref)
    for g, w in enumerate(POOL_WINDOWS):
        lanes = slice(g * POOL_GROUP, (g + 1) * POOL_GROUP)
        cnt = jnp.minimum(pos1, w).astype(F32)
        p = wins[g][POOL_HEAD:n, 0:POOL_GROUP] / cnt - ext_ref[POOL_HEAD:n, lanes]
        z = _dot(p.astype(BF16), wpool_ref[g]) * pscale_ref[:, lanes]
        zo_ref[:, lanes] = z.astype(BF16)
    y = _dot(zo_ref[:, 0:POOL_WIDTH], wout_ref[0:POOL_WIDTH, :])

    row = lax.broadcasted_iota(jnp.int32, (GLA_HEADS * C, GLA_KEY), 0)
    lane = lax.broadcasted_iota(jnp.int32, (GLA_HEADS * C, GLA_KEY), 1)
    head_mask = (row // C) == (lane // GLA_HEAD_K)
    arow = lax.broadcasted_iota(jnp.int32, (GLA_HEADS * C, C), 0)
    acol = lax.broadcasted_iota(jnp.int32, (GLA_HEADS * C, C), 1)
    causal = (arow % C) >= acol
    for c in range(nc):
        r = slice(c * C, (c + 1) * C)
        qm = jnp.where(head_mask, jnp.concatenate([qin_ref[r, :]] * GLA_HEADS, axis=0), 0.0)
        qm_ref[c] = qm
        att = lax.dot_general(qm, kin_ref[r, :], (((1,), (1,)), ((), ())), preferred_element_type=F32)
        att_ref[c] = jnp.where(causal, att, 0.0).astype(BF16)
    for c in range(nc):
        r = slice(c * C, (c + 1) * C)
        upd_ref[c] = _head_blocks(lax.dot_general(kdec_ref[r, :], v16_ref[r, :], (((0,), (0,)), ((), ())),
                                                  preferred_element_type=F32), GLA_HEAD_K)
    state = state_ref[...]
    for c in range(nc):
        r = slice(c * C, (c + 1) * C)
        o_inter = _dot(qm_ref[c], state.astype(BF16))
        for h in range(GLA_HEADS):
            go_ref[r, h * GLA_HEAD_V:(h + 1) * GLA_HEAD_V] = o_inter[h * C:(h + 1) * C, :]
        state = decay_t[:, c:c + 1] * state + upd_ref[c]
    state_ref[...] = state
    gnorm = gnorm_ref[...]
    half = GLA_HEADS // 2
    for hh in range(2):
        for h in range(hh * half, (hh + 1) * half):
            vl = slice(h * GLA_HEAD_V, (h + 1) * GLA_HEAD_V)
            for c in range(nc):
                r = slice(c * C, (c + 1) * C)
                o_h = go_ref[r, vl] + _dot(att_ref[c, h * C:(h + 1) * C, :], v16_ref[r, vl])
                zo_ref[r, POOL_WIDTH + h * GLA_HEAD_V:POOL_WIDTH + (h + 1) * GLA_HEAD_V] = _gla_out(
                    o_h, og_ref[r, vl], gnorm)
        lo = POOL_WIDTH + hh * half * GLA_HEAD_V
        y = y + _dot(zo_ref[:, lo:lo + half * GLA_HEAD_V], wout_ref[lo:lo + half * GLA_HEAD_V, :])

    o_ref[...] = x_ref[...] + gt * y

    @pl.when(j == pl.num_programs(1) - 1)
    def _():
        pool_out_ref[0] = ext_ref[n - POOL_BUF:n, :]
        gla_out_ref[0] = state_ref[...]

    ext_ref[0:POOL_HEAD, :] = ext_ref[tt:n, :]


def _mix_prompt_call(x2d, ada_p, ln, wcat, wgk, bgk, wpool, pscale, gnorm, wout, *, batch, seq):
    d = D_MODEL
    tt = TOKEN_TILE
    nt = seq // tt
    n = POOL_HEAD + tt
    return pl.pallas_call(
        _mix_prompt_body,
        out_shape=(jax.ShapeDtypeStruct((batch * seq, d), F32),
                   jax.ShapeDtypeStruct((batch, POOL_BUF, POOL_WIDTH), F32),
                   jax.ShapeDtypeStruct((batch, GLA_KEY, GLA_HEAD_V), F32)),
        grid=(batch, nt),
        in_specs=[pl.BlockSpec((tt, d), lambda b, j: (b * nt + j, 0)),
                  pl.BlockSpec((1, 1, 3 * d), lambda b, j: (b, 0, 1)),
                  _const_spec((1, d)), _const_spec(wcat.shape), _const_spec(wgk.shape),
                  _const_spec((1, GLA_KEY)), _const_spec(wpool.shape), _const_spec((1, POOL_WIDTH)),
                  _const_spec((1, GLA_HEAD_V)), _const_spec(wout.shape)],
        out_specs=(pl.BlockSpec((tt, d), lambda b, j: (b * nt + j, 0)),
                   pl.BlockSpec((1, POOL_BUF, POOL_WIDTH), lambda b, j: (b, 0, 0)),
                   pl.BlockSpec((1, GLA_KEY, GLA_HEAD_V), lambda b, j: (b, 0, 0))),
        scratch_shapes=[pltpu.VMEM((tt, d), BF16),
                        pltpu.VMEM((n, POOL_WIDTH), F32), pltpu.VMEM((n, 4 * POOL_GROUP), F32),
                        pltpu.VMEM((n, 3 * POOL_GROUP), F32), pltpu.VMEM((n, 2 * POOL_GROUP), F32),
                        pltpu.VMEM((n, POOL_GROUP), F32),
                        pltpu.VMEM((tt, GLA_KEY), F32), pltpu.VMEM((tt, GLA_KEY), F32),
                        pltpu.VMEM((tt, GLA_WIDTH), F32), pltpu.VMEM((tt, GLA_WIDTH), F32),
                        pltpu.VMEM((tt // GLA_CHUNK, GLA_HEADS * GLA_CHUNK, GLA_CHUNK), BF16),
                        pltpu.VMEM((tt, d), BF16),
                        pltpu.VMEM((GLA_KEY, GLA_HEAD_V), F32),
                        pltpu.VMEM((tt // GLA_CHUNK, GLA_KEY), F32),
                        pltpu.VMEM((tt, GLA_KEY), BF16), pltpu.VMEM((tt, GLA_KEY), BF16),
                        pltpu.VMEM((tt, GLA_KEY), BF16), pltpu.VMEM((tt, GLA_WIDTH), BF16),
                        pltpu.VMEM((tt // GLA_CHUNK, GLA_HEADS * GLA_CHUNK, GLA_KEY), BF16),
                        pltpu.VMEM((tt, GLA_WIDTH), F32),
                        pltpu.VMEM((tt // GLA_CHUNK, GLA_KEY, GLA_HEAD_V), F32)],
        compiler_params=pltpu.CompilerParams(dimension_semantics=("arbitrary", "arbitrary"),
                                             vmem_limit_bytes=VMEM_LIMIT),
        name="mix_prompt",
    )(x2d, ada_p, ln.reshape(1, d), wcat, wgk, bgk.reshape(1, GLA_KEY), wpool,
      pscale.reshape(1, POOL_WIDTH), gnorm.reshape(1, GLA_HEAD_V), wout)


def _mix_sample_body(x_ref, ada_ref, pool_ref, gla_ref, ln_ref, wcat_ref, wgk_ref, bgk_ref, wpool_ref,
                     pscale_ref, gnorm_ref, wout_ref, o_ref, pool_out_ref, gla_out_ref,
                     mod_ref, h_ref, u_ref, ext_ref, s1_ref, s2_ref, s4_ref, s8_ref, p_ref,
                     q_ref, k_ref, v_ref, og_ref, zo_ref, oint_ref, *, steps, pos0):
    rows = x_ref.shape[0]
    T = steps
    G = rows // T
    C = GLA_CHUNK
    SEG = SAMPLE_SEG

    sh, sc, gt = _modulation(ada_ref, mod_ref, rows, T)
    h_ref[...] = (_rms(x_ref[...], ln_ref[...]) * (1.0 + sc) + sh).astype(BF16)
    la = _gate_decays(h_ref, wcat_ref, wgk_ref, bgk_ref)
    _project_in(h_ref, wcat_ref, u_ref, q_ref, k_ref, v_ref, og_ref)

    ext_ref[0:POOL_HEAD, :] = jnp.zeros((POOL_HEAD, POOL_WIDTH), F32)
    for s in range(G):
        base = POOL_HEAD + s * SEG
        ext_ref[base:base + 16, :] = pool_ref[s]
        ext_ref[base + 16:base + SEG, :] = u_ref[s * T:(s + 1) * T, :]
    _window_sums(ext_ref, s1_ref, s2_ref, s4_ref, s8_ref)
    pos1 = lax.broadcasted_iota(jnp.int32, (T, POOL_GROUP), 0) + (pos0 + 1)
    wins = (s1_ref, s2_ref, s4_ref, s8_ref)
    for s in range(G):
        base = POOL_HEAD + s * SEG
        for g, w in enumerate(POOL_WINDOWS):
            lanes = slice(g * POOL_GROUP, (g + 1) * POOL_GROUP)
            cnt = jnp.minimum(pos1, w).astype(F32)
            p_ref[s * T:(s + 1) * T, lanes] = (wins[g][base + 16:base + SEG, 0:POOL_GROUP] / cnt
                                               - u_ref[s * T:(s + 1) * T, lanes])
        pool_out_ref[s] = ext_ref[base + SEG - POOL_BUF:base + SEG, :]
    for g in range(len(POOL_WINDOWS)):
        lanes = slice(g * POOL_GROUP, (g + 1) * POOL_GROUP)
        z = _dot(p_ref[:, lanes].astype(BF16), wpool_ref[g]) * pscale_ref[:, lanes]
        zo_ref[:, lanes] = z.astype(BF16)

    step = lax.broadcasted_iota(jnp.int32, (rows, GLA_KEY), 0) % T
    b = la
    shift = 1
    while shift < T:
        b = b + jnp.where(step >= shift, pltpu.roll(b, shift, axis=0), 0.0)
        shift *= 2
    b_last = jnp.broadcast_to(b.reshape(G, T, GLA_KEY)[:, T - 1:T, :], (G, T, GLA_KEY)).reshape(rows, GLA_KEY)
    kk = k_ref[...]
    q_in = q_ref[...] * jnp.exp(b) * (GLA_HEAD_K ** -0.5)
    k_in = (kk * jnp.exp(-b)).astype(BF16)
    k_dec = (kk * jnp.exp(b_last - b)).astype(BF16)
    v_all = v_ref[...].astype(BF16)
    decay_t = jnp.exp(b_last).T

    hrow = lax.broadcasted_iota(jnp.int32, (GLA_HEADS * T, GLA_KEY), 0)
    hlane = lax.broadcasted_iota(jnp.int32, (GLA_HEADS * T, GLA_KEY), 1)
    seq_head_mask = (hrow // T) == (hlane // GLA_HEAD_K)
    for s in range(G):
        r = slice(s * T, (s + 1) * T)
        state = gla_ref[s]
        qm = _head_stack(q_in[r, :], seq_head_mask)
        o_int = _dot(qm, state.astype(BF16))
        for h in range(GLA_HEADS):
            oint_ref[h, r, :] = o_int[h * T:(h + 1) * T, :]
        upd = _head_blocks(lax.dot_general(k_dec[r, :], v_all[r, :], (((0,), (0,)), ((), ())),
                                           preferred_element_type=F32), GLA_HEAD_K)
        gla_out_ref[s] = decay_t[:, s * T:s * T + 1] * state + upd

    row = lax.broadcasted_iota(jnp.int32, (GLA_HEADS * C, GLA_KEY), 0)
    lane = lax.broadcasted_iota(jnp.int32, (GLA_HEADS * C, GLA_KEY), 1)
    head_mask = (row // C) == (lane // GLA_HEAD_K)
    arow = lax.broadcasted_iota(jnp.int32, (GLA_HEADS * C, C), 0) % C
    acol = lax.broadcasted_iota(jnp.int32, (GLA_HEADS * C, C), 1)
    causal = (arow >= acol) & ((arow // T) == (acol // T))
    gnorm = gnorm_ref[...]
    for c in range(rows // C):
        r = slice(c * C, (c + 1) * C)
        qm = _head_stack(q_in[r, :], head_mask)
        att = lax.dot_general(qm, k_in[r, :], (((1,), (1,)), ((), ())), preferred_element_type=F32)
        att = jnp.where(causal, att, 0.0).astype(BF16)
        for h in range(GLA_HEADS):
            vl = slice(h * GLA_HEAD_V, (h + 1) * GLA_HEAD_V)
            o_h = _dot(att[h * C:(h + 1) * C, :], v_all[r, vl]) + oint_ref[h, r, :]
            zo_ref[r, POOL_WIDTH + h * GLA_HEAD_V:POOL_WIDTH + (h + 1) * GLA_HEAD_V] = _gla_out(
                o_h, og_ref[r, vl], gnorm)

    o_ref[...] = x_ref[...] + gt * _dot(zo_ref[...], wout_ref[...])


def _mix_sample_call(x2d, ada_s, pool16, gla_state, ln, wcat, wgk, bgk, wpool, pscale, gnorm, wout, *,
                     steps, pos0):
    d = D_MODEL
    nseq = gla_state.shape[0]
    g = 32
    rows = g * steps
    n = POOL_HEAD + g * SAMPLE_SEG
    return pl.pallas_call(
        functools.partial(_mix_sample_body, steps=steps, pos0=pos0),
        out_shape=(jax.ShapeDtypeStruct((nseq * steps, d), F32),
                   jax.ShapeDtypeStruct((nseq, POOL_BUF, POOL_WIDTH), F32),
                   jax.ShapeDtypeStruct((nseq, GLA_KEY, GLA_HEAD_V), F32)),
        grid=(nseq // g,),
        in_specs=[pl.BlockSpec((rows, d), lambda i: (i, 0)),
                  pl.BlockSpec((g, 3 * d), lambda i: (i, 1)),
                  pl.BlockSpec((g, POOL_BUF + 1, POOL_WIDTH), lambda i: (i, 0, 0)),
                  pl.BlockSpec((g, GLA_KEY, GLA_HEAD_V), lambda i: (i, 0, 0)),
                  _const_spec((1, d)), _const_spec(wcat.shape), _const_spec(wgk.shape),
                  _const_spec((1, GLA_KEY)), _const_spec(wpool.shape), _const_spec((1, POOL_WIDTH)),
                  _const_spec((1, GLA_HEAD_V)), _const_spec(wout.shape)],
        out_specs=(pl.BlockSpec((rows, d), lambda i: (i, 0)),
                   pl.BlockSpec((g, POOL_BUF, POOL_WIDTH), lambda i: (i, 0, 0)),
                   pl.BlockSpec((g, GLA_KEY, GLA_HEAD_V), lambda i: (i, 0, 0))),
        scratch_shapes=[pltpu.VMEM((rows, 3 * d), F32), pltpu.VMEM((rows, d), BF16),
                        pltpu.VMEM((rows, POOL_WIDTH), F32),
                        pltpu.VMEM((n, POOL_WIDTH), F32), pltpu.VMEM((n, 4 * POOL_GROUP), F32),
                        pltpu.VMEM((n, 3 * POOL_GROUP), F32), pltpu.VMEM((n, 2 * POOL_GROUP), F32),
                        pltpu.VMEM((n, POOL_GROUP), F32), pltpu.VMEM((rows, POOL_WIDTH), F32),
                        pltpu.VMEM((rows, GLA_KEY), F32), pltpu.VMEM((rows, GLA_KEY), F32),
                        pltpu.VMEM((rows, GLA_WIDTH), F32), pltpu.VMEM((rows, GLA_WIDTH), F32),
                        pltpu.VMEM((rows, d), BF16),
                        pltpu.VMEM((GLA_HEADS, rows, GLA_HEAD_V), F32)],
        compiler_params=pltpu.CompilerParams(dimension_semantics=("arbitrary",),
                                             vmem_limit_bytes=VMEM_LIMIT),
        name="mix_sample",
    )(x2d, ada_s, pool16, gla_state, ln.reshape(1, d), wcat, wgk, bgk.reshape(1, GLA_KEY), wpool,
      pscale.reshape(1, POOL_WIDTH), gnorm.reshape(1, GLA_HEAD_V), wout)


def _cat_in_weights(w_in):
    u, q, k, v, glr, og = jnp.split(w_in, (512, 768, 1024, 1536, 1552), axis=1)
    glr = jnp.pad(glr, ((0, 0), (0, LANES - GLA_GATE_RANK)))
    return jnp.concatenate([u, q, k, v, og, glr], axis=1).astype(BF16)


def kernel(x_prompt, x_sample, state_pool, state_gla, c_prompt, c_sample, ln_ffn1, ln_mix, ln_ffn2,
           w_ada, b_ada, w_ffn1_up, w_ffn1_down, w_in, w_gk2, b_gk2, w_pool, pool_scale, gla_norm,
           w_out, w_ffn2_up, w_ffn2_down, ln_final):
    bp, seq, d = x_prompt.shape
    bs, steps, _ = x_sample.shape
    depth = ln_ffn1.shape[0]
    xp = x_prompt.reshape(bp * seq, d)
    xs = x_sample.reshape(bs * steps, d)
    c_all = jnp.concatenate([c_prompt, c_sample], axis=0)
    pool_p, gla_p, pool_s, gla_s = [], [], [], []
    for l in range(depth):
        ada = _ada_call(c_all, w_ada[l], b_ada[l])
        ada_p = ada[:bp].reshape(bp, 1, N_ADA * d)
        ada_s = ada[bp:]
        w1u, w1d = w_ffn1_up[l].astype(BF16), w_ffn1_down[l].astype(BF16)
        w2u, w2d = w_ffn2_up[l].astype(BF16), w_ffn2_down[l].astype(BF16)
        wcat = _cat_in_weights(w_in[l])
        wgk = jnp.pad(w_gk2[l], ((0, LANES - GLA_GATE_RANK), (0, 0))).astype(BF16)
        wpool = w_pool[l].astype(BF16)
        wout = w_out[l].astype(BF16)
        lnf = ln_final if l == depth - 1 else None

        xp = _ffn_call(xp, ada_p, 0, ln_ffn1[l], w1u, w1d, None, rows_per_seq=seq, name="ffn1_prompt")
        xs = _ffn_call(xs, ada_s, 0, ln_ffn1[l], w1u, w1d, None, rows_per_seq=steps, name="ffn1_sample")

        xp, nb_p, ns_p = _mix_prompt_call(xp, ada_p, ln_mix[l], wcat, wgk, b_gk2[l], wpool, pool_scale[l],
                                          gla_norm[l], wout, batch=bp, seq=seq)
        pool16 = jnp.pad(state_pool[l], ((0, 0), (1, 0), (0, 0)))
        xs, nb_s, ns_s = _mix_sample_call(xs, ada_s, pool16, state_gla[l].reshape(bs, GLA_KEY, GLA_HEAD_V),
                                          ln_mix[l], wcat, wgk, b_gk2[l], wpool, pool_scale[l], gla_norm[l],
                                          wout, steps=steps, pos0=PAST_LEN)

        xp = _ffn_call(xp, ada_p, 2, ln_ffn2[l], w2u, w2d, lnf, rows_per_seq=seq, name="ffn2_prompt")
        xs = _ffn_call(xs, ada_s, 2, ln_ffn2[l], w2u, w2d, lnf, rows_per_seq=steps, name="ffn2_sample")
        pool_p.append(nb_p)
        gla_p.append(ns_p.reshape(bp, GLA_HEADS, GLA_HEAD_K, GLA_HEAD_V))
        pool_s.append(nb_s)
        gla_s.append(ns_s.reshape(bs, GLA_HEADS, GLA_HEAD_K, GLA_HEAD_V))
    return (xp.reshape(bp, seq, d), xs.reshape(bs, steps, d), jnp.stack(pool_p), jnp.stack(gla_p),
            jnp.stack(pool_s), jnp.stack(gla_s))
```

```python
import functools

import jax
import jax.numpy as jnp
from jax import lax
from jax.experimental import pallas as pl
from jax.experimental.pallas import tpu as pltpu

D_MODEL = 1024
POOL_WIDTH = 512
POOL_WINDOWS = (2, 4, 8, 16)
POOL_GROUP = 128
POOL_BUF = 15
GLA_WIDTH = 512
GLA_HEADS = 4
GLA_HEAD_V = 128
GLA_HEAD_K = 64
GLA_KEY = 256
GLA_GATE_RANK = 16
GLA_GATE_NORM = 16.0
GLA_CHUNK = 64
D_FF = 2816
N_ADA = 9
EPS = 1e-6
PAST_LEN = 16384

LANES = 128
FFN_CHUNK = 256
TOKEN_TILE = 512
FFN_TILE = 1024
FFN_SUB = 512
POOL_HEAD = 32
SAMPLE_SEG = 24
VMEM_LIMIT = 56 * 1024 * 1024

BF16 = jnp.bfloat16
F32 = jnp.float32

_CAT_U, _CAT_Q, _CAT_K, _CAT_V, _CAT_OG, _CAT_GLR = 0, 512, 768, 1024, 1536, 2048
_CAT_WIDTH = 2176


def _dot(a, b):
    return jnp.dot(a, b, preferred_element_type=F32)


def _const_spec(shape):
    nd = len(shape)
    return pl.BlockSpec(shape, lambda *_: (0,) * nd, pipeline_mode=pl.Buffered(1))


def _rms(x, g):
    return x * lax.rsqrt(jnp.mean(x * x, axis=-1, keepdims=True) + EPS) * g


def _log_sigmoid(x):
    return jnp.minimum(x, 0.0) - jnp.log1p(jnp.exp(-jnp.abs(x)))


def _modulation(ada_ref, mod_ref, rows, rows_per_seq):
    D = D_MODEL
    if rows_per_seq >= rows:
        ada = ada_ref[0]
        return lambda r: (ada[:, :D], ada[:, D:2 * D], ada[:, 2 * D:])
    for s in range(rows // rows_per_seq):
        mod_ref[s * rows_per_seq:(s + 1) * rows_per_seq, :] = jnp.broadcast_to(
            ada_ref[s:s + 1, :], (rows_per_seq, 3 * D))
    return lambda r: (mod_ref[r, :D], mod_ref[r, D:2 * D], mod_ref[r, 2 * D:])


def _ada_body(c_ref, w_ref, b_ref, o_ref):
    c = c_ref[...]
    a = (c * jax.nn.sigmoid(c)).astype(BF16)
    o_ref[...] = _dot(a, w_ref[...].astype(BF16)) + b_ref[...]


def _ada_call(c_all, w_ada, b_ada):
    n, d = c_all.shape
    nout = w_ada.shape[1]
    tn = 1024
    return pl.pallas_call(
        _ada_body,
        out_shape=jax.ShapeDtypeStruct((n, nout), F32),
        grid=(nout // tn,),
        in_specs=[pl.BlockSpec((n, d), lambda j: (0, 0)),
                  pl.BlockSpec((d, tn), lambda j: (0, j)),
                  pl.BlockSpec((1, tn), lambda j: (0, j))],
        out_specs=pl.BlockSpec((n, tn), lambda j: (0, j)),
        compiler_params=pltpu.CompilerParams(vmem_limit_bytes=VMEM_LIMIT),
        name="ada_proj",
    )(c_all, w_ada, b_ada.reshape(1, nout))


def _ffn_body(*refs, rows_per_seq, final):
    x_ref, ada_ref, ln_ref, wup_ref, wdn_ref = refs[:5]
    rest = refs[5:]
    if final:
        lnf_ref, rest = rest[0], rest[1:]
    o_ref, h_ref, a_ref = rest[:3]
    mod_ref = rest[3] if len(rest) > 3 else None
    rows = x_ref.shape[0]
    mod = _modulation(ada_ref, mod_ref, rows, rows_per_seq)
    subs = [slice(lo, lo + FFN_SUB) for lo in range(0, rows, FFN_SUB)]

    def norm_rows(r):
        sh, sc, _ = mod(r)
        h_ref[r, :] = (_rms(x_ref[r, :], ln_ref[...]) * (1.0 + sc) + sh).astype(BF16)

    n_chunks = D_FF // FFN_CHUNK
    piece = FFN_SUB // 8
    norm_rows(subs[0])
    for si, r in enumerate(subs):
        for c in range(n_chunks):
            lo = c * FFN_CHUNK
            gu = _dot(h_ref[r, :], wup_ref[:, 2 * lo:2 * lo + 2 * FFN_CHUNK])
            g, u = gu[:, :FFN_CHUNK], gu[:, FFN_CHUNK:]
            a_ref[r, lo:lo + FFN_CHUNK] = (g * jax.nn.sigmoid(g) * u).astype(BF16)
            if si + 1 < len(subs) and c < 8:
                nxt = subs[si + 1].start + c * piece
                norm_rows(slice(nxt, nxt + piece))
        y = x_ref[r, :] + 0.5 * mod(r)[2] * _dot(a_ref[r, :], wdn_ref[...])
        if final:
            y = _rms(y, lnf_ref[...])
        o_ref[r, :] = y


def _ffn_call(x2d, ada, sub, ln, w_up, w_down, ln_final, *, rows_per_seq, name):
    n, d = x2d.shape
    tm = min(FFN_TILE, n)
    final = ln_final is not None
    if rows_per_seq >= tm:
        tiles_per_seq = rows_per_seq // tm
        ada_spec = pl.BlockSpec((1, 1, 3 * d), lambda i: (i // tiles_per_seq, 0, sub))
        scratch_mod = []
    else:
        g = tm // rows_per_seq
        ada_spec = pl.BlockSpec((g, 3 * d), lambda i: (i, sub))
        scratch_mod = [pltpu.VMEM((tm, 3 * d), F32)]
    in_specs = [pl.BlockSpec((tm, d), lambda i: (i, 0)), ada_spec, _const_spec((1, d)),
                _const_spec(w_up.shape), _const_spec(w_down.shape)]
    args = [x2d, ada, ln.reshape(1, d), w_up, w_down]
    if final:
        in_specs.append(_const_spec((1, d)))
        args.append(ln_final.reshape(1, d))
    return pl.pallas_call(
        functools.partial(_ffn_body, rows_per_seq=rows_per_seq, final=final),
        out_shape=jax.ShapeDtypeStruct((n, d), F32),
        grid=(n // tm,),
        in_specs=in_specs,
        out_specs=pl.BlockSpec((tm, d), lambda i: (i, 0)),
        scratch_shapes=[pltpu.VMEM((tm, d), BF16), pltpu.VMEM((tm, D_FF), BF16)] + scratch_mod,
        compiler_params=pltpu.CompilerParams(dimension_semantics=("arbitrary",),
                                             vmem_limit_bytes=VMEM_LIMIT),
        name=name,
    )(*args)


def _project_in(h_ref, wcat_ref, ext_dst, q_ref, k_ref, v_ref, og_ref):
    q_ref[...] = _dot(h_ref[...], wcat_ref[:, _CAT_Q:_CAT_K])
    k_ref[...] = _dot(h_ref[...], wcat_ref[:, _CAT_K:_CAT_V])
    v_ref[...] = _dot(h_ref[...], wcat_ref[:, _CAT_V:_CAT_OG])
    ext_dst[...] = _dot(h_ref[...], wcat_ref[:, _CAT_U:_CAT_Q])
    og_ref[...] = _dot(h_ref[...], wcat_ref[:, _CAT_OG:_CAT_GLR])


def _window_sums(ext_ref, s1_ref, s2_ref, s4_ref, s8_ref):
    n = ext_ref.shape[0]
    g = POOL_GROUP
    s1_ref[8:n, :] = ext_ref[8:n, :] + ext_ref[7:n - 1, :]
    s2_ref[16:n, :] = s1_ref[16:n, g:4 * g] + s1_ref[14:n - 2, g:4 * g]
    s4_ref[24:n, :] = s2_ref[24:n, g:3 * g] + s2_ref[20:n - 4, g:3 * g]
    s8_ref[32:n, :] = s4_ref[32:n, g:2 * g] + s4_ref[24:n - 8, g:2 * g]


def _gate_decays(h_ref, wcat_ref, wgk_ref, bgk_ref):
    glr = _dot(h_ref[...], wcat_ref[:, _CAT_GLR:_CAT_WIDTH])
    gk = _dot(glr.astype(BF16), wgk_ref[...]) + bgk_ref[...]
    return _log_sigmoid(gk) * (1.0 / GLA_GATE_NORM)


def _head_stack(q_in, head_mask):
    return jnp.where(head_mask, jnp.concatenate([q_in] * GLA_HEADS, axis=0), 0.0).astype(BF16)


def _head_blocks(full, rows):
    return jnp.concatenate(
        [full[h * rows:(h + 1) * rows, h * GLA_HEAD_V:(h + 1) * GLA_HEAD_V] for h in range(GLA_HEADS)], axis=0)


def _gla_out(o_h, og_h, gnorm):
    o_n = o_h * lax.rsqrt(jnp.mean(o_h * o_h, axis=-1, keepdims=True) + EPS) * gnorm
    return (o_n * (og_h * jax.nn.sigmoid(og_h))).astype(BF16)


def _mix_prompt_body(x_ref, ada_ref, ln_ref, wcat_ref, wgk_ref, bgk_ref, wpool_ref, pscale_ref,
                     gnorm_ref, wout_ref, o_ref, pool_out_ref, gla_out_ref,
                     h_ref, ext_ref, s1_ref, s2_ref, s4_ref, s8_ref, q_ref, k_ref, v_ref, og_ref,
                     att_ref, zo_ref, state_ref, bl_ref, qin_ref, kin_ref, kdec_ref, v16_ref, qm_ref,
                     go_ref, upd_ref):
    j = pl.program_id(1)
    tt = x_ref.shape[0]
    n = POOL_HEAD + tt
    C = GLA_CHUNK

    @pl.when(j == 0)
    def _():
        ext_ref[0:POOL_HEAD, :] = jnp.zeros((POOL_HEAD, POOL_WIDTH), F32)
        state_ref[...] = jnp.zeros(state_ref.shape, F32)

    ada = ada_ref[0]
    sh, sc, gt = ada[:, :D_MODEL], ada[:, D_MODEL:2 * D_MODEL], ada[:, 2 * D_MODEL:]
    h_ref[...] = (_rms(x_ref[...], ln_ref[...]) * (1.0 + sc) + sh).astype(BF16)
    la = _gate_decays(h_ref, wcat_ref, wgk_ref, bgk_ref)
    _project_in(h_ref, wcat_ref, ext_ref.at[POOL_HEAD:n, :], q_ref, k_ref, v_ref, og_ref)

    nc = tt // C
    step = lax.broadcasted_iota(jnp.int32, (tt, GLA_KEY), 0) % C
    b = la
    shift = 1
    while shift < C:
        b = b + jnp.where(step >= shift, pltpu.roll(b, shift, axis=0), 0.0)
        shift *= 2
    for c in range(nc):
        bl_ref[c:c + 1, :] = b[(c + 1) * C - 1:(c + 1) * C, :]
    b_last = jnp.broadcast_to(bl_ref[...][:, None, :], (nc, C, GLA_KEY)).reshape(tt, GLA_KEY)
    kk = k_ref[...]
    qin_ref[...] = (q_ref[...] * jnp.exp(b) * (GLA_HEAD_K ** -0.5)).astype(BF16)
    kin_ref[...] = (kk * jnp.exp(-b)).astype(BF16)
    kdec_ref[...] = (kk * jnp.exp(b_last - b)).astype(BF16)
    v16_ref[...] = v_ref[...].astype(BF16)
    decay_t = jnp.exp(bl_ref[...]).T

    _window_sums(ext_ref, s1_ref, s2_ref, s4_ref, s8_ref)
    pos1 = lax.broadcasted_iota(jnp.int32, (tt, POOL_GROUP), 0) + (j * tt + 1)
    wins = (s1_ref, s2_ref, s4_ref, s8_ref)
    for g, w in enumerate(POOL_WINDOWS):
        lanes = slice(g * POOL_GROUP, (g + 1) * POOL_GROUP)
        cnt = jnp.minimum(pos1, w).astype(F32)
        p = wins[g][POOL_HEAD:n, 0:POOL_GROUP] / cnt - ext_ref[POOL_HEAD:n, lanes]
        z = _dot(p.astype(BF16), wpool_ref[g]) * pscale_ref[:, lanes]
        zo_ref[:, lanes] = z.astype(BF16)
    y = _dot(zo_ref[:, 0:POOL_WIDTH], wout_ref[0:POOL_WIDTH, :])

    row = lax.broadcasted_iota(jnp.int32, (GLA_HEADS * C, GLA_KEY), 0)
    lane = lax.broadcasted_iota(jnp.int32, (GLA_HEADS * C, GLA_KEY), 1)
    head_mask = (row // C) == (lane // GLA_HEAD_K)
    arow = lax.broadcasted_iota(jnp.int32, (GLA_HEADS * C, C), 0)
    acol = lax.broadcasted_iota(jnp.int32, (GLA_HEADS * C, C), 1)
    causal = (arow % C) >= acol
    for c in range(nc):
        r = slice(c * C, (c + 1) * C)
        qm = jnp.where(head_mask, jnp.concatenate([qin_ref[r, :]] * GLA_HEADS, axis=0), 0.0)
        qm_ref[c] = qm
        att = lax.dot_general(qm, kin_ref[r, :], (((1,), (1,)), ((), ())), preferred_element_type=F32)
        att_ref[c] = jnp.where(causal, att, 0.0).astype(BF16)
    for c in range(nc):
        r = slice(c * C, (c + 1) * C)
        upd_ref[c] = _head_blocks(lax.dot_general(kdec_ref[r, :], v16_ref[r, :], (((0,), (0,)), ((), ())),
                                                  preferred_element_type=F32), GLA_HEAD_K)
    state = state_ref[...]
    for c in range(nc):
        r = slice(c * C, (c + 1) * C)
        o_inter = _dot(qm_ref[c], state.astype(BF16))
        for h in range(GLA_HEADS):
            go_ref[r, h * GLA_HEAD_V:(h + 1) * GLA_HEAD_V] = o_inter[h * C:(h + 1) * C, :]
        state = decay_t[:, c:c + 1] * state + upd_ref[c]
    state_ref[...] = state
    gnorm = gnorm_ref[...]
    half = GLA_HEADS // 2
    for hh in range(2):
        for h in range(hh * half, (hh + 1) * half):
            vl = slice(h * GLA_HEAD_V, (h + 1) * GLA_HEAD_V)
            for c in range(nc):
                r = slice(c * C, (c + 1) * C)
                o_h = go_ref[r, vl] + _dot(att_ref[c, h * C:(h + 1) * C, :], v16_ref[r, vl])
                zo_ref[r, POOL_WIDTH + h * GLA_HEAD_V:POOL_WIDTH + (h + 1) * GLA_HEAD_V] = _gla_out(
                    o_h, og_ref[r, vl], gnorm)
        lo = POOL_WIDTH + hh * half * GLA_HEAD_V
        y = y + _dot(zo_ref[:, lo:lo + half * GLA_HEAD_V], wout_ref[lo:lo + half * GLA_HEAD_V, :])

    o_ref[...] = x_ref[...] + gt * y

    @pl.when(j == pl.num_programs(1) - 1)
    def _():
        pool_out_ref[0] = ext_ref[n - POOL_BUF:n, :]
        gla_out_ref[0] = state_ref[...]

    ext_ref[0:POOL_HEAD, :] = ext_ref[tt:n, :]


def _mix_prompt_call(x2d, ada_p, ln, wcat, wgk, bgk, wpool, pscale, gnorm, wout, *, batch, seq):
    d = D_MODEL
    tt = TOKEN_TILE
    nt = seq // tt
    n = POOL_HEAD + tt
    return pl.pallas_call(
        _mix_prompt_body,
        out_shape=(jax.ShapeDtypeStruct((batch * seq, d), F32),
                   jax.ShapeDtypeStruct((batch, POOL_BUF, POOL_WIDTH), F32),
                   jax.ShapeDtypeStruct((batch, GLA_KEY, GLA_HEAD_V), F32)),
        grid=(batch, nt),
        in_specs=[pl.BlockSpec((tt, d), lambda b, j: (b * nt + j, 0)),
                  pl.BlockSpec((1, 1, 3 * d), lambda b, j: (b, 0, 1)),
                  _const_spec((1, d)), _const_spec(wcat.shape), _const_spec(wgk.shape),
                  _const_spec((1, GLA_KEY)), _const_spec(wpool.shape), _const_spec((1, POOL_WIDTH)),
                  _const_spec((1, GLA_HEAD_V)), _const_spec(wout.shape)],
        out_specs=(pl.BlockSpec((tt, d), lambda b, j: (b * nt + j, 0)),
                   pl.BlockSpec((1, POOL_BUF, POOL_WIDTH), lambda b, j: (b, 0, 0)),
                   pl.BlockSpec((1, GLA_KEY, GLA_HEAD_V), lambda b, j: (b, 0, 0))),
        scratch_shapes=[pltpu.VMEM((tt, d), BF16),
                        pltpu.VMEM((n, POOL_WIDTH), F32), pltpu.VMEM((n, 4 * POOL_GROUP), F32),
                        pltpu.VMEM((n, 3 * POOL_GROUP), F32), pltpu.VMEM((n, 2 * POOL_GROUP), F32),
                        pltpu.VMEM((n, POOL_GROUP), F32),
                        pltpu.VMEM((tt, GLA_KEY), F32), pltpu.VMEM((tt, GLA_KEY), F32),
                        pltpu.VMEM((tt, GLA_WIDTH), F32), pltpu.VMEM((tt, GLA_WIDTH), F32),
                        pltpu.VMEM((tt // GLA_CHUNK, GLA_HEADS * GLA_CHUNK, GLA_CHUNK), BF16),
                        pltpu.VMEM((tt, d), BF16),
                        pltpu.VMEM((GLA_KEY, GLA_HEAD_V), F32),
                        pltpu.VMEM((tt // GLA_CHUNK, GLA_KEY), F32),
                        pltpu.VMEM((tt, GLA_KEY), BF16), pltpu.VMEM((tt, GLA_KEY), BF16),
                        pltpu.VMEM((tt, GLA_KEY), BF16), pltpu.VMEM((tt, GLA_WIDTH), BF16),
                        pltpu.VMEM((tt // GLA_CHUNK, GLA_HEADS * GLA_CHUNK, GLA_KEY), BF16),
                        pltpu.VMEM((tt, GLA_WIDTH), F32),
                        pltpu.VMEM((tt // GLA_CHUNK, GLA_KEY, GLA_HEAD_V), F32)],
        compiler_params=pltpu.CompilerParams(dimension_semantics=("arbitrary", "arbitrary"),
                                             vmem_limit_bytes=VMEM_LIMIT),
        name="mix_prompt",
    )(x2d, ada_p, ln.reshape(1, d), wcat, wgk, bgk.reshape(1, GLA_KEY), wpool,
      pscale.reshape(1, POOL_WIDTH), gnorm.reshape(1, GLA_HEAD_V), wout)


def _mix_sample_body(x_ref, ada_ref, pool_ref, gla_ref, ln_ref, wcat_ref, wgk_ref, bgk_ref, wpool_ref,
                     pscale_ref, gnorm_ref, wout_ref, o_ref, pool_out_ref, gla_out_ref,
                     mod_ref, h_ref, u_ref, ext_ref, s1_ref, s2_ref, s4_ref, s8_ref, p_ref,
                     q_ref, k_ref, v_ref, og_ref, zo_ref, oint_ref, *, steps, pos0):
    rows = x_ref.shape[0]
    T = steps
    G = rows // T
    C = GLA_CHUNK
    SEG = SAMPLE_SEG

    sh, sc, gt = _modulation(ada_ref, mod_ref, rows, T)(slice(None))
    h_ref[...] = (_rms(x_ref[...], ln_ref[...]) * (1.0 + sc) + sh).astype(BF16)
    la = _gate_decays(h_ref, wcat_ref, wgk_ref, bgk_ref)
    _project_in(h_ref, wcat_ref, u_ref, q_ref, k_ref, v_ref, og_ref)

    ext_ref[0:POOL_HEAD, :] = jnp.zeros((POOL_HEAD, POOL_WIDTH), F32)
    for s in range(G):
        base = POOL_HEAD + s * SEG
        ext_ref[base:base + 16, :] = pool_ref[s]
        ext_ref[base + 16:base + SEG, :] = u_ref[s * T:(s + 1) * T, :]
    _window_sums(ext_ref, s1_ref, s2_ref, s4_ref, s8_ref)
    pos1 = lax.broadcasted_iota(jnp.int32, (T, POOL_GROUP), 0) + (pos0 + 1)
    wins = (s1_ref, s2_ref, s4_ref, s8_ref)
    for s in range(G):
        base = POOL_HEAD + s * SEG
        for g, w in enumerate(POOL_WINDOWS):
            lanes = slice(g * POOL_GROUP, (g + 1) * POOL_GROUP)
            cnt = jnp.minimum(pos1, w).astype(F32)
            p_ref[s * T:(s + 1) * T, lanes] = (wins[g][base + 16:base + SEG, 0:POOL_GROUP] / cnt
                                               - u_ref[s * T:(s + 1) * T, lanes])
        pool_out_ref[s] = ext_ref[base + SEG - POOL_BUF:base + SEG, :]
    for g in range(len(POOL_WINDOWS)):
        lanes = slice(g * POOL_GROUP, (g + 1) * POOL_GROUP)
        z = _dot(p_ref[:, lanes].astype(BF16), wpool_ref[g]) * pscale_ref[:, lanes]
        zo_ref[:, lanes] = z.astype(BF16)

    step = lax.broadcasted_iota(jnp.int32, (rows, GLA_KEY), 0) % T
    b = la
    shift = 1
    while shift < T:
        b = b + jnp.where(step >= shift, pltpu.roll(b, shift, axis=0), 0.0)
        shift *= 2
    b_last = jnp.broadcast_to(b.reshape(G, T, GLA_KEY)[:, T - 1:T, :], (G, T, GLA_KEY)).reshape(rows, GLA_KEY)
    kk = k_ref[...]
    q_in = q_ref[...] * jnp.exp(b) * (GLA_HEAD_K ** -0.5)
    k_in = (kk * jnp.exp(-b)).astype(BF16)
    k_dec = (kk * jnp.exp(b_last - b)).astype(BF16)
    v_all = v_ref[...].astype(BF16)
    decay_t = jnp.exp(b_last).T

    hrow = lax.broadcasted_iota(jnp.int32, (GLA_HEADS * T, GLA_KEY), 0)
    hlane = lax.broadcasted_iota(jnp.int32, (GLA_HEADS * T, GLA_KEY), 1)
    seq_head_mask = (hrow // T) == (hlane // GLA_HEAD_K)
    for s in range(G):
        r = slice(s * T, (s + 1) * T)
        state = gla_ref[s]
        qm = _head_stack(q_in[r, :], seq_head_mask)
        o_int = _dot(qm, state.astype(BF16))
        for h in range(GLA_HEADS):
            oint_ref[h, r, :] = o_int[h * T:(h + 1) * T, :]
        upd = _head_blocks(lax.dot_general(k_dec[r, :], v_all[r, :], (((0,), (0,)), ((), ())),
                                           preferred_element_type=F32), GLA_HEAD_K)
        gla_out_ref[s] = decay_t[:, s * T:s * T + 1] * state + upd

    row = lax.broadcasted_iota(jnp.int32, (GLA_HEADS * C, GLA_KEY), 0)
    lane = lax.broadcasted_iota(jnp.int32, (GLA_HEADS * C, GLA_KEY), 1)
    head_mask = (row // C) == (lane // GLA_HEAD_K)
    arow = lax.broadcasted_iota(jnp.int32, (GLA_HEADS * C, C), 0) % C
    acol = lax.broadcasted_iota(jnp.int32, (GLA_HEADS * C, C), 1)
    causal = (arow >= acol) & ((arow // T) == (acol // T))
    gnorm = gnorm_ref[...]
    for c in range(rows // C):
        r = slice(c * C, (c + 1) * C)
        qm = _head_stack(q_in[r, :], head_mask)
        att = lax.dot_general(qm, k_in[r, :], (((1,), (1,)), ((), ())), preferred_element_type=F32)
        att = jnp.where(causal, att, 0.0).astype(BF16)
        for h in range(GLA_HEADS):
            vl = slice(h * GLA_HEAD_V, (h + 1) * GLA_HEAD_V)
            o_h = _dot(att[h * C:(h + 1) * C, :], v_all[r, vl]) + oint_ref[h, r, :]
            zo_ref[r, POOL_WIDTH + h * GLA_HEAD_V:POOL_WIDTH + (h + 1) * GLA_HEAD_V] = _gla_out(
                o_h, og_ref[r, vl], gnorm)

    o_ref[...] = x_ref[...] + gt * _dot(zo_ref[...], wout_ref[...])


def _mix_sample_call(x2d, ada_s, pool16, gla_state, ln, wcat, wgk, bgk, wpool, pscale, gnorm, wout, *,
                     steps, pos0):
    d = D_MODEL
    nseq = gla_state.shape[0]
    g = 32
    rows = g * steps
    n = POOL_HEAD + g * SAMPLE_SEG
    return pl.pallas_call(
        functools.partial(_mix_sample_body, steps=steps, pos0=pos0),
        out_shape=(jax.ShapeDtypeStruct((nseq * steps, d), F32),
                   jax.ShapeDtypeStruct((nseq, POOL_BUF, POOL_WIDTH), F32),
                   jax.ShapeDtypeStruct((nseq, GLA_KEY, GLA_HEAD_V), F32)),
        grid=(nseq // g,),
        in_specs=[pl.BlockSpec((rows, d), lambda i: (i, 0)),
                  pl.BlockSpec((g, 3 * d), lambda i: (i, 1)),
                  pl.BlockSpec((g, POOL_BUF + 1, POOL_WIDTH), lambda i: (i, 0, 0)),
                  pl.BlockSpec((g, GLA_KEY, GLA_HEAD_V), lambda i: (i, 0, 0)),
                  _const_spec((1, d)), _const_spec(wcat.shape), _const_spec(wgk.shape),
                  _const_spec((1, GLA_KEY)), _const_spec(wpool.shape), _const_spec((1, POOL_WIDTH)),
                  _const_spec((1, GLA_HEAD_V)), _const_spec(wout.shape)],
        out_specs=(pl.BlockSpec((rows, d), lambda i: (i, 0)),
                   pl.BlockSpec((g, POOL_BUF, POOL_WIDTH), lambda i: (i, 0, 0)),
                   pl.BlockSpec((g, GLA_KEY, GLA_HEAD_V), lambda i: (i, 0, 0))),
        scratch_shapes=[pltpu.VMEM((rows, 3 * d), F32), pltpu.VMEM((rows, d), BF16),
                        pltpu.VMEM((rows, POOL_WIDTH), F32),
                        pltpu.VMEM((n, POOL_WIDTH), F32), pltpu.VMEM((n, 4 * POOL_GROUP), F32),
                        pltpu.VMEM((n, 3 * POOL_GROUP), F32), pltpu.VMEM((n, 2 * POOL_GROUP), F32),
                        pltpu.VMEM((n, POOL_GROUP), F32), pltpu.VMEM((rows, POOL_WIDTH), F32),
                        pltpu.VMEM((rows, GLA_KEY), F32), pltpu.VMEM((rows, GLA_KEY), F32),
                        pltpu.VMEM((rows, GLA_WIDTH), F32), pltpu.VMEM((rows, GLA_WIDTH), F32),
                        pltpu.VMEM((rows, d), BF16),
                        pltpu.VMEM((GLA_HEADS, rows, GLA_HEAD_V), F32)],
        compiler_params=pltpu.CompilerParams(dimension_semantics=("arbitrary",),
                                             vmem_limit_bytes=VMEM_LIMIT),
        name="mix_sample",
    )(x2d, ada_s, pool16, gla_state, ln.reshape(1, d), wcat, wgk, bgk.reshape(1, GLA_KEY), wpool,
      pscale.reshape(1, POOL_WIDTH), gnorm.reshape(1, GLA_HEAD_V), wout)


def _cat_in_weights(w_in):
    u, q, k, v, glr, og = jnp.split(w_in, (512, 768, 1024, 1536, 1552), axis=1)
    glr = jnp.pad(glr, ((0, 0), (0, LANES - GLA_GATE_RANK)))
    return jnp.concatenate([u, q, k, v, og, glr], axis=1).astype(BF16)


def _pair_up_weights(w_up):
    d = w_up.shape[0]
    w = w_up.reshape(d, 2, D_FF // FFN_CHUNK, FFN_CHUNK)
    return jnp.swapaxes(w, 1, 2).reshape(d, 2 * D_FF).astype(BF16)


def kernel(x_prompt, x_sample, state_pool, state_gla, c_prompt, c_sample, ln_ffn1, ln_mix, ln_ffn2,
           w_ada, b_ada, w_ffn1_up, w_ffn1_down, w_in, w_gk2, b_gk2, w_pool, pool_scale, gla_norm,
           w_out, w_ffn2_up, w_ffn2_down, ln_final):
    bp, seq, d = x_prompt.shape
    bs, steps, _ = x_sample.shape
    depth = ln_ffn1.shape[0]
    xp = x_prompt.reshape(bp * seq, d)
    xs = x_sample.reshape(bs * steps, d)
    c_all = jnp.concatenate([c_prompt, c_sample], axis=0)
    pool_p, gla_p, pool_s, gla_s = [], [], [], []
    for l in range(depth):
        ada = _ada_call(c_all, w_ada[l], b_ada[l])
        ada_p = ada[:bp].reshape(bp, 1, N_ADA * d)
        ada_s = ada[bp:]
        w1u, w1d = _pair_up_weights(w_ffn1_up[l]), w_ffn1_down[l].astype(BF16)
        w2u, w2d = _pair_up_weights(w_ffn2_up[l]), w_ffn2_down[l].astype(BF16)
        wcat = _cat_in_weights(w_in[l])
        wgk = jnp.pad(w_gk2[l], ((0, LANES - GLA_GATE_RANK), (0, 0))).astype(BF16)
        wpool = w_pool[l].astype(BF16)
        wout = w_out[l].astype(BF16)
        lnf = ln_final if l == depth - 1 else None

        xp = _ffn_call(xp, ada_p, 0, ln_ffn1[l], w1u, w1d, None, rows_per_seq=seq, name="ffn1_prompt")
        xs = _ffn_call(xs, ada_s, 0, ln_ffn1[l], w1u, w1d, None, rows_per_seq=steps, name="ffn1_sample")

        xp, nb_p, ns_p = _mix_prompt_call(xp, ada_p, ln_mix[l], wcat, wgk, b_gk2[l], wpool, pool_scale[l],
                                          gla_norm[l], wout, batch=bp, seq=seq)
        pool16 = jnp.pad(state_pool[l], ((0, 0), (1, 0), (0, 0)))
        xs, nb_s, ns_s = _mix_sample_call(xs, ada_s, pool16, state_gla[l].reshape(bs, GLA_KEY, GLA_HEAD_V),
                                          ln_mix[l], wcat, wgk, b_gk2[l], wpool, pool_scale[l], gla_norm[l],
                                          wout, steps=steps, pos0=PAST_LEN)

        xp = _ffn_call(xp, ada_p, 2, ln_ffn2[l], w2u, w2d, lnf, rows_per_seq=seq, name="ffn2_prompt")
        xs = _ffn_call(xs, ada_s, 2, ln_ffn2[l], w2u, w2d, lnf, rows_per_seq=steps, name="ffn2_sample")
        pool_p.append(nb_p)
        gla_p.append(ns_p.reshape(bp, GLA_HEADS, GLA_HEAD_K, GLA_HEAD_V))
        pool_s.append(nb_s)
        gla_s.append(ns_s.reshape(bs, GLA_HEADS, GLA_HEAD_K, GLA_HEAD_V))
    return (xp.reshape(bp, seq, d), xs.reshape(bs, steps, d), jnp.stack(pool_p), jnp.stack(gla_p),
            jnp.stack(pool_s), jnp.stack(gla_s))
```

```python
import functools

import jax
import jax.numpy as jnp
from jax import lax
from jax.experimental import pallas as pl
from jax.experimental.pallas import tpu as pltpu

D_MODEL = 1024
POOL_WIDTH = 512
POOL_WINDOWS = (2, 4, 8, 16)
POOL_GROUP = 128
POOL_BUF = 15
GLA_WIDTH = 512
GLA_HEADS = 4
GLA_HEAD_V = 128
GLA_HEAD_K = 64
GLA_KEY = 256
GLA_GATE_RANK = 16
GLA_GATE_NORM = 16.0
GLA_CHUNK = 64
D_FF = 2816
N_ADA = 9
EPS = 1e-6
PAST_LEN = 16384

LANES = 128
FFN_CHUNK = 256
TOKEN_TILE = 1024
FFN_TILE = 1024
FFN_SUB = 512
POOL_HEAD = 32
SAMPLE_SEG = 24
VMEM_LIMIT = 56 * 1024 * 1024

BF16 = jnp.bfloat16
F32 = jnp.float32

_CAT_U, _CAT_Q, _CAT_K, _CAT_V, _CAT_OG, _CAT_GLR = 0, 512, 768, 1024, 1536, 2048
_CAT_WIDTH = 2176


def _dot(a, b):
    return jnp.dot(a, b, preferred_element_type=F32)


def _const_spec(shape):
    nd = len(shape)
    return pl.BlockSpec(shape, lambda *_: (0,) * nd, pipeline_mode=pl.Buffered(1))


def _rms(x, g):
    return x * lax.rsqrt(jnp.mean(x * x, axis=-1, keepdims=True) + EPS) * g


def _log_sigmoid(x):
    return jnp.minimum(x, 0.0) - jnp.log1p(jnp.exp(-jnp.abs(x)))


def _modulation(ada_ref, mod_ref, rows, rows_per_seq):
    D = D_MODEL
    if rows_per_seq >= rows:
        ada = ada_ref[0]
        return lambda r: (ada[:, :D], ada[:, D:2 * D], ada[:, 2 * D:])
    for s in range(rows // rows_per_seq):
        mod_ref[s * rows_per_seq:(s + 1) * rows_per_seq, :] = jnp.broadcast_to(
            ada_ref[s:s + 1, :], (rows_per_seq, 3 * D))
    return lambda r: (mod_ref[r, :D], mod_ref[r, D:2 * D], mod_ref[r, 2 * D:])


def _ada_body(c_ref, w_ref, b_ref, o_ref):
    c = c_ref[...]
    a = (c * jax.nn.sigmoid(c)).astype(BF16)
    o_ref[...] = _dot(a, w_ref[...].astype(BF16)) + b_ref[...]


def _ada_call(c_all, w_ada, b_ada):
    n, d = c_all.shape
    nout = w_ada.shape[1]
    tn = 1024
    return pl.pallas_call(
        _ada_body,
        out_shape=jax.ShapeDtypeStruct((n, nout), F32),
        grid=(nout // tn,),
        in_specs=[pl.BlockSpec((n, d), lambda j: (0, 0)),
                  pl.BlockSpec((d, tn), lambda j: (0, j)),
                  pl.BlockSpec((1, tn), lambda j: (0, j))],
        out_specs=pl.BlockSpec((n, tn), lambda j: (0, j)),
        compiler_params=pltpu.CompilerParams(vmem_limit_bytes=VMEM_LIMIT),
        name="ada_proj",
    )(c_all, w_ada, b_ada.reshape(1, nout))


def _ffn_body(*refs, rows_per_seq, final, n_conv):
    x_ref, ada_ref, ln_ref, wup_ref, wdn_ref = refs[:5]
    rest = refs[5:]
    if final:
        lnf_ref, rest = rest[0], rest[1:]
    conv_in, rest = rest[:n_conv], rest[n_conv:]
    o_ref, rest = rest[0], rest[1:]
    conv_out, rest = rest[:n_conv], rest[n_conv:]
    h_ref, a_ref = rest[:2]
    mod_ref = rest[2] if len(rest) > 2 else None
    rows = x_ref.shape[0]
    mod = _modulation(ada_ref, mod_ref, rows, rows_per_seq)
    subs = [slice(lo, lo + FFN_SUB) for lo in range(0, rows, FFN_SUB)]

    def norm_rows(r):
        sh, sc, _ = mod(r)
        h_ref[r, :] = (_rms(x_ref[r, :], ln_ref[...]) * (1.0 + sc) + sh).astype(BF16)

    n_chunks = D_FF // FFN_CHUNK
    piece = FFN_SUB // 8
    norm_rows(subs[0])
    for si, r in enumerate(subs):
        for c in range(n_chunks):
            lo = c * FFN_CHUNK
            g = _dot(h_ref[r, :], wup_ref[:, lo:lo + FFN_CHUNK])
            u = _dot(h_ref[r, :], wup_ref[:, D_FF + lo:D_FF + lo + FFN_CHUNK])
            a_ref[r, lo:lo + FFN_CHUNK] = (g * jax.nn.sigmoid(g) * u).astype(BF16)
            if si + 1 < len(subs) and c < 8:
                nxt = subs[si + 1].start + c * piece
                norm_rows(slice(nxt, nxt + piece))
        y = x_ref[r, :] + 0.5 * mod(r)[2] * _dot(a_ref[r, :], wdn_ref[...])
        if final:
            y = _rms(y, lnf_ref[...])
        o_ref[r, :] = y
    for src, dst in zip(conv_in, conv_out):
        dst[...] = src[...].astype(BF16)


def _ffn_call(x2d, ada, sub, ln, w_up, w_down, ln_final, *, rows_per_seq, name, convert=()):
    n, d = x2d.shape
    tm = min(FFN_TILE, n)
    steps = n // tm
    final = ln_final is not None
    if rows_per_seq >= tm:
        tiles_per_seq = rows_per_seq // tm
        ada_spec = pl.BlockSpec((1, 1, 3 * d), lambda i: (i // tiles_per_seq, 0, sub))
        scratch_mod = []
    else:
        g = tm // rows_per_seq
        ada_spec = pl.BlockSpec((g, 3 * d), lambda i: (i, sub))
        scratch_mod = [pltpu.VMEM((tm, 3 * d), F32)]
    in_specs = [pl.BlockSpec((tm, d), lambda i: (i, 0)), ada_spec, _const_spec((1, d)),
                _const_spec(w_up.shape), _const_spec(w_down.shape)]
    args = [x2d, ada, ln.reshape(1, d), w_up, w_down]
    if final:
        in_specs.append(_const_spec((1, d)))
        args.append(ln_final.reshape(1, d))
    out_shape = [jax.ShapeDtypeStruct((n, d), F32)]
    out_specs = [pl.BlockSpec((tm, d), lambda i: (i, 0))]
    for w in convert:
        slab = (w.shape[0] // steps, w.shape[1])
        assert slab[0] * steps == w.shape[0] and slab[0] % 16 == 0, (w.shape, steps)
        in_specs.append(pl.BlockSpec(slab, lambda i: (i, 0)))
        args.append(w)
        out_shape.append(jax.ShapeDtypeStruct(w.shape, BF16))
        out_specs.append(pl.BlockSpec(slab, lambda i: (i, 0)))
    outs = pl.pallas_call(
        functools.partial(_ffn_body, rows_per_seq=rows_per_seq, final=final, n_conv=len(convert)),
        out_shape=tuple(out_shape),
        grid=(steps,),
        in_specs=in_specs,
        out_specs=tuple(out_specs),
        scratch_shapes=[pltpu.VMEM((tm, d), BF16), pltpu.VMEM((tm, D_FF), BF16)] + scratch_mod,
        compiler_params=pltpu.CompilerParams(dimension_semantics=("arbitrary",),
                                             vmem_limit_bytes=VMEM_LIMIT),
        name=name,
    )(*args)
    return outs[0], outs[1:]


def _project_in(h_ref, wcat_ref, ext_dst, q_ref, k_ref, v_ref, og_ref):
    q_ref[...] = _dot(h_ref[...], wcat_ref[:, _CAT_Q:_CAT_K])
    k_ref[...] = _dot(h_ref[...], wcat_ref[:, _CAT_K:_CAT_V])
    v_ref[...] = _dot(h_ref[...], wcat_ref[:, _CAT_V:_CAT_OG])
    ext_dst[...] = _dot(h_ref[...], wcat_ref[:, _CAT_U:_CAT_Q])
    og_ref[...] = _dot(h_ref[...], wcat_ref[:, _CAT_OG:_CAT_GLR])


def _window_sums(ext_ref, s1_ref, s2_ref, s4_ref, s8_ref):
    n = ext_ref.shape[0]
    g = POOL_GROUP
    s1_ref[8:n, :] = ext_ref[8:n, :] + ext_ref[7:n - 1, :]
    s2_ref[16:n, :] = s1_ref[16:n, g:4 * g] + s1_ref[14:n - 2, g:4 * g]
    s4_ref[24:n, :] = s2_ref[24:n, g:3 * g] + s2_ref[20:n - 4, g:3 * g]
    s8_ref[32:n, :] = s4_ref[32:n, g:2 * g] + s4_ref[24:n - 8, g:2 * g]


def _gate_decays(h_ref, wcat_ref, wgk_ref, bgk_ref):
    glr = _dot(h_ref[...], wcat_ref[:, _CAT_GLR:_CAT_WIDTH])
    gk = _dot(glr.astype(BF16), wgk_ref[...]) + bgk_ref[...]
    return _log_sigmoid(gk) * (1.0 / GLA_GATE_NORM)


def _head_stack(q_in, head_mask):
    return jnp.where(head_mask, jnp.concatenate([q_in] * GLA_HEADS, axis=0), 0.0).astype(BF16)


def _head_blocks(full, rows):
    return jnp.concatenate(
        [full[h * rows:(h + 1) * rows, h * GLA_HEAD_V:(h + 1) * GLA_HEAD_V] for h in range(GLA_HEADS)], axis=0)


def _gla_out(o_h, og_h, gnorm):
    o_n = o_h * lax.rsqrt(jnp.mean(o_h * o_h, axis=-1, keepdims=True) + EPS) * gnorm
    return (o_n * (og_h * jax.nn.sigmoid(og_h))).astype(BF16)


def _mix_prompt_body(x_ref, ada_ref, ln_ref, wcat_ref, wgk_ref, bgk_ref, wpool_ref, pscale_ref,
                     gnorm_ref, wout_ref, o_ref, pool_out_ref, gla_out_ref,
                     h_ref, ext_ref, s1_ref, s2_ref, s4_ref, s8_ref, q_ref, k_ref, v_ref, og_ref,
                     att_ref, zo_ref, state_ref, bl_ref, qin_ref, kin_ref, kdec_ref, v16_ref, qm_ref,
                     go_ref, upd_ref):
    j = pl.program_id(1)
    tt = x_ref.shape[0]
    n = POOL_HEAD + tt
    C = GLA_CHUNK

    @pl.when(j == 0)
    def _():
        ext_ref[0:POOL_HEAD, :] = jnp.zeros((POOL_HEAD, POOL_WIDTH), F32)
        state_ref[...] = jnp.zeros(state_ref.shape, F32)

    ada = ada_ref[0]
    sh, sc, gt = ada[:, :D_MODEL], ada[:, D_MODEL:2 * D_MODEL], ada[:, 2 * D_MODEL:]
    h_ref[...] = (_rms(x_ref[...], ln_ref[...]) * (1.0 + sc) + sh).astype(BF16)
    la = _gate_decays(h_ref, wcat_ref, wgk_ref, bgk_ref)
    _project_in(h_ref, wcat_ref, ext_ref.at[POOL_HEAD:n, :], q_ref, k_ref, v_ref, og_ref)

    nc = tt // C
    step = lax.broadcasted_iota(jnp.int32, (tt, GLA_KEY), 0) % C
    b = la
    shift = 1
    while shift < C:
        b = b + jnp.where(step >= shift, pltpu.roll(b, shift, axis=0), 0.0)
        shift *= 2
    for c in range(nc):
        bl_ref[c:c + 1, :] = b[(c + 1) * C - 1:(c + 1) * C, :]
    b_last = jnp.broadcast_to(bl_ref[...][:, None, :], (nc, C, GLA_KEY)).reshape(tt, GLA_KEY)
    kk = k_ref[...]
    qin_ref[...] = (q_ref[...] * jnp.exp(b) * (GLA_HEAD_K ** -0.5)).astype(BF16)
    kin_ref[...] = (kk * jnp.exp(-b)).astype(BF16)
    kdec_ref[...] = (kk * jnp.exp(b_last - b)).astype(BF16)
    v16_ref[...] = v_ref[...].astype(BF16)
    decay_t = jnp.exp(bl_ref[...]).T

    _window_sums(ext_ref, s1_ref, s2_ref, s4_ref, s8_ref)
    pos1 = lax.broadcasted_iota(jnp.int32, (tt, POOL_GROUP), 0) + (j * tt + 1)
    wins = (s1_ref, s2_ref, s4_ref, s8_ref)
    for g, w in enumerate(POOL_WINDOWS):
        lanes = slice(g * POOL_GROUP, (g + 1) * POOL_GROUP)
        cnt = jnp.minimum(pos1, w).astype(F32)
        p = wins[g][POOL_HEAD:n, 0:POOL_GROUP] / cnt - ext_ref[POOL_HEAD:n, lanes]
        z = _dot(p.astype(BF16), wpool_ref[g]) * pscale_ref[:, lanes]
        zo_ref[:, lanes] = z.astype(BF16)
    y = _dot(zo_ref[:, 0:POOL_WIDTH], wout_ref[0:POOL_WIDTH, :])

    row = lax.broadcasted_iota(jnp.int32, (GLA_HEADS * C, GLA_KEY), 0)
    lane = lax.broadcasted_iota(jnp.int32, (GLA_HEADS * C, GLA_KEY), 1)
    head_mask = (row // C) == (lane // GLA_HEAD_K)
    arow = lax.broadcasted_iota(jnp.int32, (GLA_HEADS * C, C), 0)
    acol = lax.broadcasted_iota(jnp.int32, (GLA_HEADS * C, C), 1)
    causal = (arow % C) >= acol
    for c in range(nc):
        r = slice(c * C, (c + 1) * C)
        qm = jnp.where(head_mask, jnp.concatenate([qin_ref[r, :]] * GLA_HEADS, axis=0), 0.0)
        qm_ref[c] = qm
        att = lax.dot_general(qm, kin_ref[r, :], (((1,), (1,)), ((), ())), preferred_element_type=F32)
        att_ref[c] = jnp.where(causal, att, 0.0).astype(BF16)
    for c in range(nc):
        r = slice(c * C, (c + 1) * C)
        upd_ref[c] = _head_blocks(lax.dot_general(kdec_ref[r, :], v16_ref[r, :], (((0,), (0,)), ((), ())),
                                                  preferred_element_type=F32), GLA_HEAD_K)
    state = state_ref[...]
    for c in range(nc):
        r = slice(c * C, (c + 1) * C)
        o_inter = _dot(qm_ref[c], state.astype(BF16))
        for h in range(GLA_HEADS):
            go_ref[r, h * GLA_HEAD_V:(h + 1) * GLA_HEAD_V] = o_inter[h * C:(h + 1) * C, :]
        state = decay_t[:, c:c + 1] * state + upd_ref[c]
    state_ref[...] = state
    gnorm = gnorm_ref[...]
    half = GLA_HEADS // 2
    for hh in range(2):
        for h in range(hh * half, (hh + 1) * half):
            vl = slice(h * GLA_HEAD_V, (h + 1) * GLA_HEAD_V)
            for c in range(nc):
                r = slice(c * C, (c + 1) * C)
                o_h = go_ref[r, vl] + _dot(att_ref[c, h * C:(h + 1) * C, :], v16_ref[r, vl])
                zo_ref[r, POOL_WIDTH + h * GLA_HEAD_V:POOL_WIDTH + (h + 1) * GLA_HEAD_V] = _gla_out(
                    o_h, og_ref[r, vl], gnorm)
        lo = POOL_WIDTH + hh * half * GLA_HEAD_V
        y = y + _dot(zo_ref[:, lo:lo + half * GLA_HEAD_V], wout_ref[lo:lo + half * GLA_HEAD_V, :])

    o_ref[...] = x_ref[...] + gt * y

    @pl.when(j == pl.num_programs(1) - 1)
    def _():
        pool_out_ref[0] = ext_ref[n - POOL_BUF:n, :]
        gla_out_ref[0] = state_ref[...]

    ext_ref[0:POOL_HEAD, :] = ext_ref[tt:n, :]


def _mix_prompt_call(x2d, ada_p, ln, wcat, wgk, bgk, wpool, pscale, gnorm, wout, *, batch, seq):
    d = D_MODEL
    tt = TOKEN_TILE
    nt = seq // tt
    n = POOL_HEAD + tt
    return pl.pallas_call(
        _mix_prompt_body,
        out_shape=(jax.ShapeDtypeStruct((batch * seq, d), F32),
                   jax.ShapeDtypeStruct((batch, POOL_BUF, POOL_WIDTH), F32),
                   jax.ShapeDtypeStruct((batch, GLA_KEY, GLA_HEAD_V), F32)),
        grid=(batch, nt),
        in_specs=[pl.BlockSpec((tt, d), lambda b, j: (b * nt + j, 0)),
                  pl.BlockSpec((1, 1, 3 * d), lambda b, j: (b, 0, 1)),
                  _const_spec((1, d)), _const_spec(wcat.shape), _const_spec(wgk.shape),
                  _const_spec((1, GLA_KEY)), _const_spec(wpool.shape), _const_spec((1, POOL_WIDTH)),
                  _const_spec((1, GLA_HEAD_V)), _const_spec(wout.shape)],
        out_specs=(pl.BlockSpec((tt, d), lambda b, j: (b * nt + j, 0)),
                   pl.BlockSpec((1, POOL_BUF, POOL_WIDTH), lambda b, j: (b, 0, 0)),
                   pl.BlockSpec((1, GLA_KEY, GLA_HEAD_V), lambda b, j: (b, 0, 0))),
        scratch_shapes=[pltpu.VMEM((tt, d), BF16),
                        pltpu.VMEM((n, POOL_WIDTH), F32), pltpu.VMEM((n, 4 * POOL_GROUP), F32),
                        pltpu.VMEM((n, 3 * POOL_GROUP), F32), pltpu.VMEM((n, 2 * POOL_GROUP), F32),
                        pltpu.VMEM((n, POOL_GROUP), F32),
                        pltpu.VMEM((tt, GLA_KEY), F32), pltpu.VMEM((tt, GLA_KEY), F32),
                        pltpu.VMEM((tt, GLA_WIDTH), F32), pltpu.VMEM((tt, GLA_WIDTH), F32),
                        pltpu.VMEM((tt // GLA_CHUNK, GLA_HEADS * GLA_CHUNK, GLA_CHUNK), BF16),
                        pltpu.VMEM((tt, d), BF16),
                        pltpu.VMEM((GLA_KEY, GLA_HEAD_V), F32),
                        pltpu.VMEM((tt // GLA_CHUNK, GLA_KEY), F32),
                        pltpu.VMEM((tt, GLA_KEY), BF16), pltpu.VMEM((tt, GLA_KEY), BF16),
                        pltpu.VMEM((tt, GLA_KEY), BF16), pltpu.VMEM((tt, GLA_WIDTH), BF16),
                        pltpu.VMEM((tt // GLA_CHUNK, GLA_HEADS * GLA_CHUNK, GLA_KEY), BF16),
                        pltpu.VMEM((tt, GLA_WIDTH), F32),
                        pltpu.VMEM((tt // GLA_CHUNK, GLA_KEY, GLA_HEAD_V), F32)],
        compiler_params=pltpu.CompilerParams(dimension_semantics=("arbitrary", "arbitrary"),
                                             vmem_limit_bytes=VMEM_LIMIT),
        name="mix_prompt",
    )(x2d, ada_p, ln.reshape(1, d), wcat, wgk, bgk.reshape(1, GLA_KEY), wpool,
      pscale.reshape(1, POOL_WIDTH), gnorm.reshape(1, GLA_HEAD_V), wout)


def _mix_sample_body(x_ref, ada_ref, pool_ref, gla_ref, ln_ref, wcat_ref, wgk_ref, bgk_ref, wpool_ref,
                     pscale_ref, gnorm_ref, wout_ref, o_ref, pool_out_ref, gla_out_ref,
                     mod_ref, h_ref, u_ref, ext_ref, s1_ref, s2_ref, s4_ref, s8_ref, p_ref,
                     q_ref, k_ref, v_ref, og_ref, zo_ref, oint_ref, *, steps, pos0):
    rows = x_ref.shape[0]
    T = steps
    G = rows // T
    C = GLA_CHUNK
    SEG = SAMPLE_SEG

    sh, sc, gt = _modulation(ada_ref, mod_ref, rows, T)(slice(None))
    h_ref[...] = (_rms(x_ref[...], ln_ref[...]) * (1.0 + sc) + sh).astype(BF16)
    la = _gate_decays(h_ref, wcat_ref, wgk_ref, bgk_ref)
    _project_in(h_ref, wcat_ref, u_ref, q_ref, k_ref, v_ref, og_ref)

    ext_ref[0:POOL_HEAD, :] = jnp.zeros((POOL_HEAD, POOL_WIDTH), F32)
    for s in range(G):
        base = POOL_HEAD + s * SEG
        ext_ref[base:base + 16, :] = pool_ref[s]
        ext_ref[base + 16:base + SEG, :] = u_ref[s * T:(s + 1) * T, :]
    _window_sums(ext_ref, s1_ref, s2_ref, s4_ref, s8_ref)
    pos1 = lax.broadcasted_iota(jnp.int32, (T, POOL_GROUP), 0) + (pos0 + 1)
    wins = (s1_ref, s2_ref, s4_ref, s8_ref)
    for s in range(G):
        base = POOL_HEAD + s * SEG
        for g, w in enumerate(POOL_WINDOWS):
            lanes = slice(g * POOL_GROUP, (g + 1) * POOL_GROUP)
            cnt = jnp.minimum(pos1, w).astype(F32)
            p_ref[s * T:(s + 1) * T, lanes] = (wins[g][base + 16:base + SEG, 0:POOL_GROUP] / cnt
                                               - u_ref[s * T:(s + 1) * T, lanes])
        pool_out_ref[s] = ext_ref[base + SEG - POOL_BUF:base + SEG, :]
    for g in range(len(POOL_WINDOWS)):
        lanes = slice(g * POOL_GROUP, (g + 1) * POOL_GROUP)
        z = _dot(p_ref[:, lanes].astype(BF16), wpool_ref[g]) * pscale_ref[:, lanes]
        zo_ref[:, lanes] = z.astype(BF16)

    step = lax.broadcasted_iota(jnp.int32, (rows, GLA_KEY), 0) % T
    b = la
    shift = 1
    while shift < T:
        b = b + jnp.where(step >= shift, pltpu.roll(b, shift, axis=0), 0.0)
        shift *= 2
    b_last = jnp.broadcast_to(b.reshape(G, T, GLA_KEY)[:, T - 1:T, :], (G, T, GLA_KEY)).reshape(rows, GLA_KEY)
    kk = k_ref[...]
    q_in = q_ref[...] * jnp.exp(b) * (GLA_HEAD_K ** -0.5)
    k_in = (kk * jnp.exp(-b)).astype(BF16)
    k_dec = (kk * jnp.exp(b_last - b)).astype(BF16)
    v_all = v_ref[...].astype(BF16)
    decay_t = jnp.exp(b_last).T

    hrow = lax.broadcasted_iota(jnp.int32, (GLA_HEADS * T, GLA_KEY), 0)
    hlane = lax.broadcasted_iota(jnp.int32, (GLA_HEADS * T, GLA_KEY), 1)
    seq_head_mask = (hrow // T) == (hlane // GLA_HEAD_K)
    for s in range(G):
        r = slice(s * T, (s + 1) * T)
        state = gla_ref[s]
        qm = _head_stack(q_in[r, :], seq_head_mask)
        o_int = _dot(qm, state.astype(BF16))
        for h in range(GLA_HEADS):
            oint_ref[h, r, :] = o_int[h * T:(h + 1) * T, :]
        upd = _head_blocks(lax.dot_general(k_dec[r, :], v_all[r, :], (((0,), (0,)), ((), ())),
                                           preferred_element_type=F32), GLA_HEAD_K)
        gla_out_ref[s] = decay_t[:, s * T:s * T + 1] * state + upd

    row = lax.broadcasted_iota(jnp.int32, (GLA_HEADS * C, GLA_KEY), 0)
    lane = lax.broadcasted_iota(jnp.int32, (GLA_HEADS * C, GLA_KEY), 1)
    head_mask = (row // C) == (lane // GLA_HEAD_K)
    arow = lax.broadcasted_iota(jnp.int32, (GLA_HEADS * C, C), 0) % C
    acol = lax.broadcasted_iota(jnp.int32, (GLA_HEADS * C, C), 1)
    causal = (arow >= acol) & ((arow // T) == (acol // T))
    gnorm = gnorm_ref[...]
    for c in range(rows // C):
        r = slice(c * C, (c + 1) * C)
        qm = _head_stack(q_in[r, :], head_mask)
        att = lax.dot_general(qm, k_in[r, :], (((1,), (1,)), ((), ())), preferred_element_type=F32)
        att = jnp.where(causal, att, 0.0).astype(BF16)
        for h in range(GLA_HEADS):
            vl = slice(h * GLA_HEAD_V, (h + 1) * GLA_HEAD_V)
            o_h = _dot(att[h * C:(h + 1) * C, :], v_all[r, vl]) + oint_ref[h, r, :]
            zo_ref[r, POOL_WIDTH + h * GLA_HEAD_V:POOL_WIDTH + (h + 1) * GLA_HEAD_V] = _gla_out(
                o_h, og_ref[r, vl], gnorm)

    o_ref[...] = x_ref[...] + gt * _dot(zo_ref[...], wout_ref[...])


def _mix_sample_call(x2d, ada_s, pool16, gla_state, ln, wcat, wgk, bgk, wpool, pscale, gnorm, wout, *,
                     steps, pos0):
    d = D_MODEL
    nseq = gla_state.shape[0]
    g = 32
    rows = g * steps
    n = POOL_HEAD + g * SAMPLE_SEG
    return pl.pallas_call(
        functools.partial(_mix_sample_body, steps=steps, pos0=pos0),
        out_shape=(jax.ShapeDtypeStruct((nseq * steps, d), F32),
                   jax.ShapeDtypeStruct((nseq, POOL_BUF, POOL_WIDTH), F32),
                   jax.ShapeDtypeStruct((nseq, GLA_KEY, GLA_HEAD_V), F32)),
        grid=(nseq // g,),
        in_specs=[pl.BlockSpec((rows, d), lambda i: (i, 0)),
                  pl.BlockSpec((g, 3 * d), lambda i: (i, 1)),
                  pl.BlockSpec((g, POOL_BUF + 1, POOL_WIDTH), lambda i: (i, 0, 0)),
                  pl.BlockSpec((g, GLA_KEY, GLA_HEAD_V), lambda i: (i, 0, 0)),
                  _const_spec((1, d)), _const_spec(wcat.shape), _const_spec(wgk.shape),
                  _const_spec((1, GLA_KEY)), _const_spec(wpool.shape), _const_spec((1, POOL_WIDTH)),
                  _const_spec((1, GLA_HEAD_V)), _const_spec(wout.shape)],
        out_specs=(pl.BlockSpec((rows, d), lambda i: (i, 0)),
                   pl.BlockSpec((g, POOL_BUF, POOL_WIDTH), lambda i: (i, 0, 0)),
                   pl.BlockSpec((g, GLA_KEY, GLA_HEAD_V), lambda i: (i, 0, 0))),
        scratch_shapes=[pltpu.VMEM((rows, 3 * d), F32), pltpu.VMEM((rows, d), BF16),
                        pltpu.VMEM((rows, POOL_WIDTH), F32),
                        pltpu.VMEM((n, POOL_WIDTH), F32), pltpu.VMEM((n, 4 * POOL_GROUP), F32),
                        pltpu.VMEM((n, 3 * POOL_GROUP), F32), pltpu.VMEM((n, 2 * POOL_GROUP), F32),
                        pltpu.VMEM((n, POOL_GROUP), F32), pltpu.VMEM((rows, POOL_WIDTH), F32),
                        pltpu.VMEM((rows, GLA_KEY), F32), pltpu.VMEM((rows, GLA_KEY), F32),
                        pltpu.VMEM((rows, GLA_WIDTH), F32), pltpu.VMEM((rows, GLA_WIDTH), F32),
                        pltpu.VMEM((rows, d), BF16),
                        pltpu.VMEM((GLA_HEADS, rows, GLA_HEAD_V), F32)],
        compiler_params=pltpu.CompilerParams(dimension_semantics=("arbitrary",),
                                             vmem_limit_bytes=VMEM_LIMIT),
        name="mix_sample",
    )(x2d, ada_s, pool16, gla_state, ln.reshape(1, d), wcat, wgk, bgk.reshape(1, GLA_KEY), wpool,
      pscale.reshape(1, POOL_WIDTH), gnorm.reshape(1, GLA_HEAD_V), wout)


def _cat_in_weights(w_in):
    u, q, k, v, glr, og = jnp.split(w_in, (512, 768, 1024, 1536, 1552), axis=1)
    glr = jnp.pad(glr, ((0, 0), (0, LANES - GLA_GATE_RANK)))
    return jnp.concatenate([u, q, k, v, og, glr], axis=1).astype(BF16)


def kernel(x_prompt, x_sample, state_pool, state_gla, c_prompt, c_sample, ln_ffn1, ln_mix, ln_ffn2,
           w_ada, b_ada, w_ffn1_up, w_ffn1_down, w_in, w_gk2, b_gk2, w_pool, pool_scale, gla_norm,
           w_out, w_ffn2_up, w_ffn2_down, ln_final):
    bp, seq, d = x_prompt.shape
    bs, steps, _ = x_sample.shape
    depth = ln_ffn1.shape[0]
    xp = x_prompt.reshape(bp * seq, d)
    xs = x_sample.reshape(bs * steps, d)
    c_all = jnp.concatenate([c_prompt, c_sample], axis=0)
    pool_p, gla_p, pool_s, gla_s = [], [], [], []
    for l in range(depth):
        ada = _ada_call(c_all, w_ada[l], b_ada[l])
        ada_p = ada[:bp].reshape(bp, 1, N_ADA * d)
        ada_s = ada[bp:]
        w1u, w1d = w_ffn1_up[l].astype(BF16), w_ffn1_down[l].astype(BF16)
        wcat = _cat_in_weights(w_in[l])
        wgk = jnp.pad(w_gk2[l], ((0, LANES - GLA_GATE_RANK), (0, 0))).astype(BF16)
        wpool = w_pool[l].astype(BF16)
        lnf = ln_final if l == depth - 1 else None

        xp, (w2u, w2d, wout) = _ffn_call(xp, ada_p, 0, ln_ffn1[l], w1u, w1d, None, rows_per_seq=seq,
                                         name="ffn1_prompt", convert=(w_ffn2_up[l], w_ffn2_down[l], w_out[l]))
        xs, _ = _ffn_call(xs, ada_s, 0, ln_ffn1[l], w1u, w1d, None, rows_per_seq=steps, name="ffn1_sample")

        xp, nb_p, ns_p = _mix_prompt_call(xp, ada_p, ln_mix[l], wcat, wgk, b_gk2[l], wpool, pool_scale[l],
                                          gla_norm[l], wout, batch=bp, seq=seq)
        pool16 = jnp.pad(state_pool[l], ((0, 0), (1, 0), (0, 0)))
        xs, nb_s, ns_s = _mix_sample_call(xs, ada_s, pool16, state_gla[l].reshape(bs, GLA_KEY, GLA_HEAD_V),
                                          ln_mix[l], wcat, wgk, b_gk2[l], wpool, pool_scale[l], gla_norm[l],
                                          wout, steps=steps, pos0=PAST_LEN)

        xp, _ = _ffn_call(xp, ada_p, 2, ln_ffn2[l], w2u, w2d, lnf, rows_per_seq=seq, name="ffn2_prompt")
        xs, _ = _ffn_call(xs, ada_s, 2, ln_ffn2[l], w2u, w2d, lnf, rows_per_seq=steps, name="ffn2_sample")
        pool_p.append(nb_p)
        gla_p.append(ns_p.reshape(bp, GLA_HEADS, GLA_HEAD_K, GLA_HEAD_V))
        pool_s.append(nb_s)
        gla_s.append(ns_s.reshape(bs, GLA_HEADS, GLA_HEAD_K, GLA_HEAD_V))
    return (xp.reshape(bp, seq, d), xs.reshape(bs, steps, d), jnp.stack(pool_p), jnp.stack(gla_p),
            jnp.stack(pool_s), jnp.stack(gla_s))
```

```python
import functools

import jax
import jax.numpy as jnp
from jax import lax
from jax.experimental import pallas as pl
from jax.experimental.pallas import tpu as pltpu

D_MODEL = 1024
POOL_WIDTH = 512
POOL_WINDOWS = (2, 4, 8, 16)
N_POOL_GROUPS = len(POOL_WINDOWS)
POOL_GROUP = 128
POOL_BUF = 15
GLA_WIDTH = 512
GLA_HEADS = 4
GLA_HEAD_V = 128
GLA_HEAD_K = 64
GLA_KEY = 256
GLA_GATE_RANK = 16
GLA_GATE_NORM = 16.0
GLA_CHUNK = 64
D_FF = 2816
N_ADA = 9
EPS = 1e-6
PAST_LEN = 16384

LANES = 128
FFN_CHUNK = 256
TOKEN_TILE = 1024
FFN_TILE = 1024
FFN_SUB = 512
MIX_SUB = 512
POOL_HEAD = 32
SAMPLE_SEG = 24
VMEM_LIMIT = 56 * 1024 * 1024

BF16 = jnp.bfloat16
F32 = jnp.float32

_CAT_U, _CAT_Q, _CAT_K, _CAT_V, _CAT_OG, _CAT_GLR = 0, 512, 768, 1024, 1536, 2048
_CAT_WIDTH = 2176


def _dot(a, b):
    return jnp.dot(a, b, preferred_element_type=F32)


def _const_spec(shape):
    nd = len(shape)
    return pl.BlockSpec(shape, lambda *_: (0,) * nd, pipeline_mode=pl.Buffered(1))


def _rms(x, g):
    return x * lax.rsqrt(jnp.mean(x * x, axis=-1, keepdims=True) + EPS) * g


def _log_sigmoid(x):
    return jnp.minimum(x, 0.0) - jnp.log1p(jnp.exp(-jnp.abs(x)))


def _modulation(ada_ref, mod_ref, rows, rows_per_seq):
    D = D_MODEL
    if rows_per_seq >= rows:
        ada = ada_ref[0]
        return lambda r: (ada[:, :D], ada[:, D:2 * D], ada[:, 2 * D:])
    for s in range(rows // rows_per_seq):
        mod_ref[s * rows_per_seq:(s + 1) * rows_per_seq, :] = jnp.broadcast_to(
            ada_ref[s:s + 1, :], (rows_per_seq, 3 * D))
    return lambda r: (mod_ref[r, :D], mod_ref[r, D:2 * D], mod_ref[r, 2 * D:])


def _ada_body(c_ref, w_ref, b_ref, o_ref):
    c = c_ref[...]
    a = (c * jax.nn.sigmoid(c)).astype(BF16)
    o_ref[...] = _dot(a, w_ref[...].astype(BF16)) + b_ref[...]


def _ada_call(c_all, w_ada, b_ada):
    n, d = c_all.shape
    nout = w_ada.shape[1]
    tn = 1024
    return pl.pallas_call(
        _ada_body,
        out_shape=jax.ShapeDtypeStruct((n, nout), F32),
        grid=(nout // tn,),
        in_specs=[pl.BlockSpec((n, d), lambda j: (0, 0)),
                  pl.BlockSpec((d, tn), lambda j: (0, j)),
                  pl.BlockSpec((1, tn), lambda j: (0, j))],
        out_specs=pl.BlockSpec((n, tn), lambda j: (0, j)),
        compiler_params=pltpu.CompilerParams(vmem_limit_bytes=VMEM_LIMIT),
        name="ada_proj",
    )(c_all, w_ada, b_ada.reshape(1, nout))


def _cast_in_weights(src_ref, dst_ref):
    src = src_ref[...]
    dst_ref[:, _CAT_U:_CAT_OG] = src[:, 0:1536].astype(BF16)
    dst_ref[:, _CAT_OG:_CAT_GLR] = src[:, 1552:2064].astype(BF16)
    glr = src[:, 1536:1552].astype(BF16)
    dst_ref[:, _CAT_GLR:_CAT_WIDTH] = jnp.concatenate(
        [glr, jnp.zeros((src.shape[0], LANES - GLA_GATE_RANK), BF16)], axis=1)


def _ffn_body(*refs, rows_per_seq, final, conv_kinds):
    n_conv = len(conv_kinds)
    x_ref, ada_ref, ln_ref, wup_ref, wdn_ref = refs[:5]
    rest = refs[5:]
    if final:
        lnf_ref, rest = rest[0], rest[1:]
    conv_in, rest = rest[:n_conv], rest[n_conv:]
    o_ref, rest = rest[0], rest[1:]
    conv_out, rest = rest[:n_conv], rest[n_conv:]
    h_ref, a_ref = rest[:2]
    mod_ref = rest[2] if len(rest) > 2 else None
    rows = x_ref.shape[0]
    mod = _modulation(ada_ref, mod_ref, rows, rows_per_seq)
    subs = [slice(lo, lo + FFN_SUB) for lo in range(0, rows, FFN_SUB)]

    def norm_rows(r):
        sh, sc, _ = mod(r)
        h_ref[r, :] = (_rms(x_ref[r, :], ln_ref[...]) * (1.0 + sc) + sh).astype(BF16)

    n_chunks = D_FF // FFN_CHUNK
    piece = FFN_SUB // 8
    norm_rows(subs[0])
    for si, r in enumerate(subs):
        for c in range(n_chunks):
            lo = c * FFN_CHUNK
            g = _dot(h_ref[r, :], wup_ref[:, lo:lo + FFN_CHUNK])
            u = _dot(h_ref[r, :], wup_ref[:, D_FF + lo:D_FF + lo + FFN_CHUNK])
            a_ref[r, lo:lo + FFN_CHUNK] = (g * jax.nn.sigmoid(g) * u).astype(BF16)
            if si + 1 < len(subs) and c < 8:
                nxt = subs[si + 1].start + c * piece
                norm_rows(slice(nxt, nxt + piece))
        y = x_ref[r, :] + 0.5 * mod(r)[2] * _dot(a_ref[r, :], wdn_ref[...])
        if final:
            y = _rms(y, lnf_ref[...])
        o_ref[r, :] = y
    for kind, src, dst in zip(conv_kinds, conv_in, conv_out):
        if kind == "w_in":
            _cast_in_weights(src, dst)
        else:
            dst[...] = src[...].astype(BF16)


def _ffn_call(x2d, ada, sub, ln, w_up, w_down, ln_final, *, rows_per_seq, name, ada_row0=0, convert=()):
    n, d = x2d.shape
    tm = min(FFN_TILE, n)
    steps = n // tm
    final = ln_final is not None
    if rows_per_seq >= tm:
        tiles_per_seq = rows_per_seq // tm
        ada_spec = pl.BlockSpec((1, 1, 3 * d), lambda i: (ada_row0 + i // tiles_per_seq, 0, sub))
        scratch_mod = []
    else:
        g = tm // rows_per_seq
        ada_spec = pl.BlockSpec((g, 3 * d), lambda i: (i, sub))
        scratch_mod = [pltpu.VMEM((tm, 3 * d), F32)]
    in_specs = [pl.BlockSpec((tm, d), lambda i: (i, 0)), ada_spec, _const_spec((1, d)),
                _const_spec(w_up.shape), _const_spec(w_down.shape)]
    args = [x2d, ada, ln.reshape(1, d), w_up, w_down]
    if final:
        in_specs.append(_const_spec((1, d)))
        args.append(ln_final.reshape(1, d))
    out_shape = [jax.ShapeDtypeStruct((n, d), F32)]
    out_specs = [pl.BlockSpec((tm, d), lambda i: (i, 0))]
    for kind, w in convert:
        slab = w.shape[0] // steps
        assert slab * steps == w.shape[0] and slab % 16 == 0, (w.shape, steps)
        cols = _CAT_WIDTH if kind == "w_in" else w.shape[1]
        in_specs.append(pl.BlockSpec((slab, w.shape[1]), lambda i: (i, 0)))
        args.append(w)
        out_shape.append(jax.ShapeDtypeStruct((w.shape[0], cols), BF16))
        out_specs.append(pl.BlockSpec((slab, cols), lambda i: (i, 0)))
    outs = pl.pallas_call(
        functools.partial(_ffn_body, rows_per_seq=rows_per_seq, final=final,
                          conv_kinds=tuple(kind for kind, _ in convert)),
        out_shape=tuple(out_shape),
        grid=(steps,),
        in_specs=in_specs,
        out_specs=tuple(out_specs),
        scratch_shapes=[pltpu.VMEM((tm, d), BF16), pltpu.VMEM((tm, D_FF), BF16)] + scratch_mod,
        compiler_params=pltpu.CompilerParams(dimension_semantics=("arbitrary",),
                                             vmem_limit_bytes=VMEM_LIMIT),
        name=name,
    )(*args)
    return outs[0], outs[1:]


def _project_in(h_ref, r, wcat_ref, wgk_ref, bgk_ref, u_dst, q_ref, k_ref, v_ref, og_ref, between=None):
    between = between or (lambda k: None)
    h = h_ref[r, :]
    glr = _dot(h, wcat_ref[:, _CAT_GLR:_CAT_WIDTH])
    q_ref[r, :] = _dot(h, wcat_ref[:, _CAT_Q:_CAT_K])
    between(0), between(1)
    k_ref[r, :] = _dot(h, wcat_ref[:, _CAT_K:_CAT_V])
    between(2), between(3)
    gk = _dot(glr.astype(BF16), wgk_ref[...]) + bgk_ref[...]
    v_ref[r, :] = _dot(h, wcat_ref[:, _CAT_V:_CAT_OG])
    between(4), between(5)
    u_dst[...] = _dot(h, wcat_ref[:, _CAT_U:_CAT_Q])
    between(6), between(7)
    og_ref[r, :] = _dot(h, wcat_ref[:, _CAT_OG:_CAT_GLR])
    return _log_sigmoid(gk) * (1.0 / GLA_GATE_NORM)


def _window_sums(ext_ref, s1_ref, s2_ref, s4_ref, s8_ref):
    n = ext_ref.shape[0]
    g = POOL_GROUP
    s1_ref[8:n, :] = ext_ref[8:n, :] + ext_ref[7:n - 1, :]
    s2_ref[16:n, :] = s1_ref[16:n, g:4 * g] + s1_ref[14:n - 2, g:4 * g]
    s4_ref[24:n, :] = s2_ref[24:n, g:3 * g] + s2_ref[20:n - 4, g:3 * g]
    s8_ref[32:n, :] = s4_ref[32:n, g:2 * g] + s4_ref[24:n - 8, g:2 * g]


def _head_stack(q_in, head_mask):
    return jnp.where(head_mask, jnp.concatenate([q_in] * GLA_HEADS, axis=0), 0.0).astype(BF16)


def _head_blocks(full, rows):
    return jnp.concatenate(
        [full[h * rows:(h + 1) * rows, h * GLA_HEAD_V:(h + 1) * GLA_HEAD_V] for h in range(GLA_HEADS)], axis=0)


def _gla_out(o_h, og_h, gnorm):
    o_n = o_h * lax.rsqrt(jnp.mean(o_h * o_h, axis=-1, keepdims=True) + EPS) * gnorm
    return (o_n * (og_h * jax.nn.sigmoid(og_h))).astype(BF16)


def _mix_prompt_body(x_ref, ada_ref, ln_ref, wcat_ref, wgk_ref, bgk_ref, wpool_ref, pscale_ref,
                     gnorm_ref, wout_ref, o_ref, pool_out_ref, gla_out_ref,
                     h_ref, ext_ref, s1_ref, s2_ref, s4_ref, s8_ref, q_ref, k_ref, v_ref, og_ref,
                     att_ref, zo_ref, state_ref, bl_ref, qin_ref, kin_ref, kdec_ref, v16_ref, qm_ref,
                     go_ref, upd_ref):
    j = pl.program_id(1)
    tt = x_ref.shape[0]
    n = POOL_HEAD + tt
    C = GLA_CHUNK

    @pl.when(j == 0)
    def _():
        ext_ref[0:POOL_HEAD, :] = jnp.zeros((POOL_HEAD, POOL_WIDTH), F32)
        state_ref[...] = jnp.zeros(state_ref.shape, F32)

    ada = ada_ref[0]
    sh, sc, gt = ada[:, :D_MODEL], ada[:, D_MODEL:2 * D_MODEL], ada[:, 2 * D_MODEL:]

    def norm_rows(r):
        h_ref[r, :] = (_rms(x_ref[r, :], ln_ref[...]) * (1.0 + sc) + sh).astype(BF16)

    subs = [slice(lo, lo + MIX_SUB) for lo in range(0, tt, MIX_SUB)]
    piece = MIX_SUB // 8
    norm_rows(subs[0])
    la_parts = []
    for si, r in enumerate(subs):
        def between(k, si=si):
            if si + 1 < len(subs):
                lo = subs[si + 1].start + k * piece
                norm_rows(slice(lo, lo + piece))
        la_parts.append(_project_in(h_ref, r, wcat_ref, wgk_ref, bgk_ref,
                                    ext_ref.at[POOL_HEAD + r.start:POOL_HEAD + r.stop, :],
                                    q_ref, k_ref, v_ref, og_ref, between))
    la = jnp.concatenate(la_parts, axis=0)

    nc = tt // C
    step = lax.broadcasted_iota(jnp.int32, (tt, GLA_KEY), 0) % C
    b = la
    shift = 1
    while shift < C:
        b = b + jnp.where(step >= shift, pltpu.roll(b, shift, axis=0), 0.0)
        shift *= 2
    for c in range(nc):
        bl_ref[c:c + 1, :] = b[(c + 1) * C - 1:(c + 1) * C, :]
    b_last = jnp.broadcast_to(bl_ref[...][:, None, :], (nc, C, GLA_KEY)).reshape(tt, GLA_KEY)
    kk = k_ref[...]
    qin_ref[...] = (q_ref[...] * jnp.exp(b) * (GLA_HEAD_K ** -0.5)).astype(BF16)
    kin_ref[...] = (kk * jnp.exp(-b)).astype(BF16)
    kdec_ref[...] = (kk * jnp.exp(b_last - b)).astype(BF16)
    v16_ref[...] = v_ref[...].astype(BF16)
    decay_t = jnp.exp(bl_ref[...]).T

    _window_sums(ext_ref, s1_ref, s2_ref, s4_ref, s8_ref)
    pos1 = lax.broadcasted_iota(jnp.int32, (tt, POOL_GROUP), 0) + (j * tt + 1)
    wins = (s1_ref, s2_ref, s4_ref, s8_ref)
    for g, w in enumerate(POOL_WINDOWS):
        lanes = slice(g * POOL_GROUP, (g + 1) * POOL_GROUP)
        cnt = jnp.minimum(pos1, w).astype(F32)
        p = wins[g][POOL_HEAD:n, 0:POOL_GROUP] / cnt - ext_ref[POOL_HEAD:n, lanes]
        z = _dot(p.astype(BF16), wpool_ref[g]) * pscale_ref[:, lanes]
        zo_ref[:, lanes] = z.astype(BF16)

    row = lax.broadcasted_iota(jnp.int32, (GLA_HEADS * C, GLA_KEY), 0)
    lane = lax.broadcasted_iota(jnp.int32, (GLA_HEADS * C, GLA_KEY), 1)
    head_mask = (row // C) == (lane // GLA_HEAD_K)
    arow = lax.broadcasted_iota(jnp.int32, (GLA_HEADS * C, C), 0)
    acol = lax.broadcasted_iota(jnp.int32, (GLA_HEADS * C, C), 1)
    causal = (arow % C) >= acol
    for c in range(nc):
        r = slice(c * C, (c + 1) * C)
        qm = jnp.where(head_mask, jnp.concatenate([qin_ref[r, :]] * GLA_HEADS, axis=0), 0.0)
        qm_ref[c] = qm
        att = lax.dot_general(qm, kin_ref[r, :], (((1,), (1,)), ((), ())), preferred_element_type=F32)
        att_ref[c] = jnp.where(causal, att, 0.0).astype(BF16)
    for c in range(nc):
        r = slice(c * C, (c + 1) * C)
        upd_ref[c] = _head_blocks(lax.dot_general(kdec_ref[r, :], v16_ref[r, :], (((0,), (0,)), ((), ())),
                                                  preferred_element_type=F32), GLA_HEAD_K)
    state = state_ref[...]
    for c in range(nc):
        r = slice(c * C, (c + 1) * C)
        o_inter = _dot(qm_ref[c], state.astype(BF16))
        for h in range(GLA_HEADS):
            go_ref[r, h * GLA_HEAD_V:(h + 1) * GLA_HEAD_V] = o_inter[h * C:(h + 1) * C, :]
        state = decay_t[:, c:c + 1] * state + upd_ref[c]
    state_ref[...] = state
    gnorm = gnorm_ref[...]
    half = GLA_HEADS // 2
    for hh in range(2):
        for h in range(hh * half, (hh + 1) * half):
            vl = slice(h * GLA_HEAD_V, (h + 1) * GLA_HEAD_V)
            for c in range(nc):
                r = slice(c * C, (c + 1) * C)
                o_h = go_ref[r, vl] + _dot(att_ref[c, h * C:(h + 1) * C, :], v16_ref[r, vl])
                zo_ref[r, POOL_WIDTH + h * GLA_HEAD_V:POOL_WIDTH + (h + 1) * GLA_HEAD_V] = _gla_out(
                    o_h, og_ref[r, vl], gnorm)
        if hh == 0:
            y = _dot(zo_ref[:, 0:POOL_WIDTH], wout_ref[0:POOL_WIDTH, :])
        lo = POOL_WIDTH + hh * half * GLA_HEAD_V
        y = y + _dot(zo_ref[:, lo:lo + half * GLA_HEAD_V], wout_ref[lo:lo + half * GLA_HEAD_V, :])

    o_ref[...] = x_ref[...] + gt * y

    @pl.when(j == pl.num_programs(1) - 1)
    def _():
        pool_out_ref[0] = ext_ref[n - POOL_BUF:n, :]
        gla_out_ref[0] = state_ref[...]

    ext_ref[0:POOL_HEAD, :] = ext_ref[tt:n, :]


def _mix_prompt_call(x2d, ada_p, ln, wcat, wgk, bgk, wpool, pscale, gnorm, wout, *, batch, seq, ada_row0):
    d = D_MODEL
    tt = TOKEN_TILE
    nt = seq // tt
    n = POOL_HEAD + tt
    return pl.pallas_call(
        _mix_prompt_body,
        out_shape=(jax.ShapeDtypeStruct((batch * seq, d), F32),
                   jax.ShapeDtypeStruct((batch, POOL_BUF, POOL_WIDTH), F32),
                   jax.ShapeDtypeStruct((batch, GLA_KEY, GLA_HEAD_V), F32)),
        grid=(batch, nt),
        in_specs=[pl.BlockSpec((tt, d), lambda b, j: (b * nt + j, 0)),
                  pl.BlockSpec((1, 1, 3 * d), lambda b, j: (ada_row0 + b, 0, 1)),
                  _const_spec((1, d)), _const_spec(wcat.shape), _const_spec(wgk.shape),
                  _const_spec((1, GLA_KEY)), _const_spec(wpool.shape), _const_spec((1, POOL_WIDTH)),
                  _const_spec((1, GLA_HEAD_V)), _const_spec(wout.shape)],
        out_specs=(pl.BlockSpec((tt, d), lambda b, j: (b * nt + j, 0)),
                   pl.BlockSpec((1, POOL_BUF, POOL_WIDTH), lambda b, j: (b, 0, 0)),
                   pl.BlockSpec((1, GLA_KEY, GLA_HEAD_V), lambda b, j: (b, 0, 0))),
        scratch_shapes=[pltpu.VMEM((tt, d), BF16),
                        pltpu.VMEM((n, POOL_WIDTH), F32), pltpu.VMEM((n, 4 * POOL_GROUP), F32),
                        pltpu.VMEM((n, 3 * POOL_GROUP), F32), pltpu.VMEM((n, 2 * POOL_GROUP), F32),
                        pltpu.VMEM((n, POOL_GROUP), F32),
                        pltpu.VMEM((tt, GLA_KEY), F32), pltpu.VMEM((tt, GLA_KEY), F32),
                        pltpu.VMEM((tt, GLA_WIDTH), F32), pltpu.VMEM((tt, GLA_WIDTH), F32),
                        pltpu.VMEM((tt // GLA_CHUNK, GLA_HEADS * GLA_CHUNK, GLA_CHUNK), BF16),
                        pltpu.VMEM((tt, d), BF16),
                        pltpu.VMEM((GLA_KEY, GLA_HEAD_V), F32),
                        pltpu.VMEM((tt // GLA_CHUNK, GLA_KEY), F32),
                        pltpu.VMEM((tt, GLA_KEY), BF16), pltpu.VMEM((tt, GLA_KEY), BF16),
                        pltpu.VMEM((tt, GLA_KEY), BF16), pltpu.VMEM((tt, GLA_WIDTH), BF16),
                        pltpu.VMEM((tt // GLA_CHUNK, GLA_HEADS * GLA_CHUNK, GLA_KEY), BF16),
                        pltpu.VMEM((tt, GLA_WIDTH), F32),
                        pltpu.VMEM((tt // GLA_CHUNK, GLA_KEY, GLA_HEAD_V), F32)],
        compiler_params=pltpu.CompilerParams(dimension_semantics=("arbitrary", "arbitrary"),
                                             vmem_limit_bytes=VMEM_LIMIT),
        name="mix_prompt",
    )(x2d, ada_p, ln.reshape(1, d), wcat, wgk, bgk.reshape(1, GLA_KEY), wpool,
      pscale.reshape(1, POOL_WIDTH), gnorm.reshape(1, GLA_HEAD_V), wout)


def _mix_sample_body(x_ref, ada_ref, pool_ref, gla_ref, ln_ref, wcat_ref, wgk_ref, bgk_ref, wpool_ref,
                     pscale_ref, gnorm_ref, wout_ref, o_ref, pool_out_ref, gla_out_ref,
                     mod_ref, h_ref, u_ref, ext_ref, s1_ref, s2_ref, s4_ref, s8_ref, p_ref,
                     q_ref, k_ref, v_ref, og_ref, zo_ref, oint_ref, *, steps, pos0):
    rows = x_ref.shape[0]
    T = steps
    G = rows // T
    C = GLA_CHUNK
    SEG = SAMPLE_SEG

    sh, sc, gt = _modulation(ada_ref, mod_ref, rows, T)(slice(None))
    h_ref[...] = (_rms(x_ref[...], ln_ref[...]) * (1.0 + sc) + sh).astype(BF16)
    la = _project_in(h_ref, slice(0, rows), wcat_ref, wgk_ref, bgk_ref, u_ref, q_ref, k_ref, v_ref, og_ref)

    ext_ref[0:POOL_HEAD, :] = jnp.zeros((POOL_HEAD, POOL_WIDTH), F32)
    for s in range(G):
        base = POOL_HEAD + s * SEG
        ext_ref[base:base + 16, :] = pool_ref[s]
        ext_ref[base + 16:base + SEG, :] = u_ref[s * T:(s + 1) * T, :]
    _window_sums(ext_ref, s1_ref, s2_ref, s4_ref, s8_ref)
    pos1 = lax.broadcasted_iota(jnp.int32, (T, POOL_GROUP), 0) + (pos0 + 1)
    wins = (s1_ref, s2_ref, s4_ref, s8_ref)
    for s in range(G):
        base = POOL_HEAD + s * SEG
        for g, w in enumerate(POOL_WINDOWS):
            lanes = slice(g * POOL_GROUP, (g + 1) * POOL_GROUP)
            cnt = jnp.minimum(pos1, w).astype(F32)
            p_ref[s * T:(s + 1) * T, lanes] = (wins[g][base + 16:base + SEG, 0:POOL_GROUP] / cnt
                                               - u_ref[s * T:(s + 1) * T, lanes])
        pool_out_ref[s] = ext_ref[base + SEG - POOL_BUF:base + SEG, :]
    for g in range(len(POOL_WINDOWS)):
        lanes = slice(g * POOL_GROUP, (g + 1) * POOL_GROUP)
        z = _dot(p_ref[:, lanes].astype(BF16), wpool_ref[g]) * pscale_ref[:, lanes]
        zo_ref[:, lanes] = z.astype(BF16)

    step = lax.broadcasted_iota(jnp.int32, (rows, GLA_KEY), 0) % T
    b = la
    shift = 1
    while shift < T:
        b = b + jnp.where(step >= shift, pltpu.roll(b, shift, axis=0), 0.0)
        shift *= 2
    b_last = jnp.broadcast_to(b.reshape(G, T, GLA_KEY)[:, T - 1:T, :], (G, T, GLA_KEY)).reshape(rows, GLA_KEY)
    kk = k_ref[...]
    q_in = q_ref[...] * jnp.exp(b) * (GLA_HEAD_K ** -0.5)
    k_in = (kk * jnp.exp(-b)).astype(BF16)
    k_dec = (kk * jnp.exp(b_last - b)).astype(BF16)
    v_all = v_ref[...].astype(BF16)
    decay_t = jnp.exp(b_last).T

    hrow = lax.broadcasted_iota(jnp.int32, (GLA_HEADS * T, GLA_KEY), 0)
    hlane = lax.broadcasted_iota(jnp.int32, (GLA_HEADS * T, GLA_KEY), 1)
    seq_head_mask = (hrow // T) == (hlane // GLA_HEAD_K)
    for s in range(G):
        r = slice(s * T, (s + 1) * T)
        state = gla_ref[s]
        qm = _head_stack(q_in[r, :], seq_head_mask)
        o_int = _dot(qm, state.astype(BF16))
        for h in range(GLA_HEADS):
            oint_ref[h, r, :] = o_int[h * T:(h + 1) * T, :]
        upd = _head_blocks(lax.dot_general(k_dec[r, :], v_all[r, :], (((0,), (0,)), ((), ())),
                                           preferred_element_type=F32), GLA_HEAD_K)
        gla_out_ref[s] = decay_t[:, s * T:s * T + 1] * state + upd

    row = lax.broadcasted_iota(jnp.int32, (GLA_HEADS * C, GLA_KEY), 0)
    lane = lax.broadcasted_iota(jnp.int32, (GLA_HEADS * C, GLA_KEY), 1)
    head_mask = (row // C) == (lane // GLA_HEAD_K)
    arow = lax.broadcasted_iota(jnp.int32, (GLA_HEADS * C, C), 0) % C
    acol = lax.broadcasted_iota(jnp.int32, (GLA_HEADS * C, C), 1)
    causal = (arow >= acol) & ((arow // T) == (acol // T))
    gnorm = gnorm_ref[...]
    for c in range(rows // C):
        r = slice(c * C, (c + 1) * C)
        qm = _head_stack(q_in[r, :], head_mask)
        att = lax.dot_general(qm, k_in[r, :], (((1,), (1,)), ((), ())), preferred_element_type=F32)
        att = jnp.where(causal, att, 0.0).astype(BF16)
        for h in range(GLA_HEADS):
            vl = slice(h * GLA_HEAD_V, (h + 1) * GLA_HEAD_V)
            o_h = _dot(att[h * C:(h + 1) * C, :], v_all[r, vl]) + oint_ref[h, r, :]
            zo_ref[r, POOL_WIDTH + h * GLA_HEAD_V:POOL_WIDTH + (h + 1) * GLA_HEAD_V] = _gla_out(
                o_h, og_ref[r, vl], gnorm)

    o_ref[...] = x_ref[...] + gt * _dot(zo_ref[...], wout_ref[...])


def _mix_sample_call(x2d, ada_s, pool16, gla_state, ln, wcat, wgk, bgk, wpool, pscale, gnorm, wout, *,
                     steps, pos0):
    d = D_MODEL
    nseq = gla_state.shape[0]
    g = 32
    rows = g * steps
    n = POOL_HEAD + g * SAMPLE_SEG
    return pl.pallas_call(
        functools.partial(_mix_sample_body, steps=steps, pos0=pos0),
        out_shape=(jax.ShapeDtypeStruct((nseq * steps, d), F32),
                   jax.ShapeDtypeStruct((nseq, POOL_BUF, POOL_WIDTH), F32),
                   jax.ShapeDtypeStruct((nseq, GLA_KEY, GLA_HEAD_V), F32)),
        grid=(nseq // g,),
        in_specs=[pl.BlockSpec((rows, d), lambda i: (i, 0)),
                  pl.BlockSpec((g, 3 * d), lambda i: (i, 1)),
                  pl.BlockSpec((g, POOL_BUF + 1, POOL_WIDTH), lambda i: (i, 0, 0)),
                  pl.BlockSpec((g, GLA_KEY, GLA_HEAD_V), lambda i: (i, 0, 0)),
                  _const_spec((1, d)), _const_spec(wcat.shape), _const_spec(wgk.shape),
                  _const_spec((1, GLA_KEY)), _const_spec(wpool.shape), _const_spec((1, POOL_WIDTH)),
                  _const_spec((1, GLA_HEAD_V)), _const_spec(wout.shape)],
        out_specs=(pl.BlockSpec((rows, d), lambda i: (i, 0)),
                   pl.BlockSpec((g, POOL_BUF, POOL_WIDTH), lambda i: (i, 0, 0)),
                   pl.BlockSpec((g, GLA_KEY, GLA_HEAD_V), lambda i: (i, 0, 0))),
        scratch_shapes=[pltpu.VMEM((rows, 3 * d), F32), pltpu.VMEM((rows, d), BF16),
                        pltpu.VMEM((rows, POOL_WIDTH), F32),
                        pltpu.VMEM((n, POOL_WIDTH), F32), pltpu.VMEM((n, 4 * POOL_GROUP), F32),
                        pltpu.VMEM((n, 3 * POOL_GROUP), F32), pltpu.VMEM((n, 2 * POOL_GROUP), F32),
                        pltpu.VMEM((n, POOL_GROUP), F32), pltpu.VMEM((rows, POOL_WIDTH), F32),
                        pltpu.VMEM((rows, GLA_KEY), F32), pltpu.VMEM((rows, GLA_KEY), F32),
                        pltpu.VMEM((rows, GLA_WIDTH), F32), pltpu.VMEM((rows, GLA_WIDTH), F32),
                        pltpu.VMEM((rows, d), BF16),
                        pltpu.VMEM((GLA_HEADS, rows, GLA_HEAD_V), F32)],
        compiler_params=pltpu.CompilerParams(dimension_semantics=("arbitrary",),
                                             vmem_limit_bytes=VMEM_LIMIT),
        name="mix_sample",
    )(x2d, ada_s, pool16, gla_state, ln.reshape(1, d), wcat, wgk, bgk.reshape(1, GLA_KEY), wpool,
      pscale.reshape(1, POOL_WIDTH), gnorm.reshape(1, GLA_HEAD_V), wout)


def kernel(x_prompt, x_sample, state_pool, state_gla, c_prompt, c_sample, ln_ffn1, ln_mix, ln_ffn2,
           w_ada, b_ada, w_ffn1_up, w_ffn1_down, w_in, w_gk2, b_gk2, w_pool, pool_scale, gla_norm,
           w_out, w_ffn2_up, w_ffn2_down, ln_final):
    bp, seq, d = x_prompt.shape
    bs, steps, _ = x_sample.shape
    depth = ln_ffn1.shape[0]
    xp = x_prompt.reshape(bp * seq, d)
    xs = x_sample.reshape(bs * steps, d)
    c_all = jnp.concatenate([c_sample, c_prompt], axis=0)
    pool_p, gla_p, pool_s, gla_s = [], [], [], []
    for l in range(depth):
        ada_s = _ada_call(c_all, w_ada[l], b_ada[l])
        ada_p = ada_s.reshape(bs + bp, 1, N_ADA * d)
        w1u, w1d = w_ffn1_up[l].astype(BF16), w_ffn1_down[l].astype(BF16)
        wgk = jnp.pad(w_gk2[l], ((0, LANES - GLA_GATE_RANK), (0, 0))).astype(BF16)
        lnf = ln_final if l == depth - 1 else None

        xp, (w2u, w2d, wout, wcat, wpool) = _ffn_call(
            xp, ada_p, 0, ln_ffn1[l], w1u, w1d, None, rows_per_seq=seq, name="ffn1_prompt", ada_row0=bs,
            convert=(("plain", w_ffn2_up[l]), ("plain", w_ffn2_down[l]), ("plain", w_out[l]), ("w_in", w_in[l]),
                     ("plain", w_pool[l].reshape(N_POOL_GROUPS * POOL_GROUP, POOL_GROUP))))
        wpool = wpool.reshape(N_POOL_GROUPS, POOL_GROUP, POOL_GROUP)
        xs, _ = _ffn_call(xs, ada_s, 0, ln_ffn1[l], w1u, w1d, None, rows_per_seq=steps, name="ffn1_sample")

        xp, nb_p, ns_p = _mix_prompt_call(xp, ada_p, ln_mix[l], wcat, wgk, b_gk2[l], wpool, pool_scale[l],
                                          gla_norm[l], wout, batch=bp, seq=seq, ada_row0=bs)
        pool16 = jnp.pad(state_pool[l], ((0, 0), (1, 0), (0, 0)))
        xs, nb_s, ns_s = _mix_sample_call(xs, ada_s, pool16, state_gla[l].reshape(bs, GLA_KEY, GLA_HEAD_V),
                                          ln_mix[l], wcat, wgk, b_gk2[l], wpool, pool_scale[l], gla_norm[l],
                                          wout, steps=steps, pos0=PAST_LEN)

        xp, _ = _ffn_call(xp, ada_p, 2, ln_ffn2[l], w2u, w2d, lnf, rows_per_seq=seq, name="ffn2_prompt",
                          ada_row0=bs)
        xs, _ = _ffn_call(xs, ada_s, 2, ln_ffn2[l], w2u, w2d, lnf, rows_per_seq=steps, name="ffn2_sample")
        pool_p.append(nb_p)
        gla_p.append(ns_p.reshape(bp, GLA_HEADS, GLA_HEAD_K, GLA_HEAD_V))
        pool_s.append(nb_s)
        gla_s.append(ns_s.reshape(bs, GLA_HEADS, GLA_HEAD_K, GLA_HEAD_V))
    return (xp.reshape(bp, seq, d), xs.reshape(bs, steps, d), jnp.stack(pool_p), jnp.stack(gla_p),
            jnp.stack(pool_s), jnp.stack(gla_s))
```

```python
import functools

import jax
import jax.numpy as jnp
from jax import lax
from jax.experimental import pallas as pl
from jax.experimental.pallas import tpu as pltpu

D_MODEL = 1024
POOL_WIDTH = 512
POOL_WINDOWS = (2, 4, 8, 16)
N_POOL_GROUPS = len(POOL_WINDOWS)
POOL_GROUP = 128
POOL_BUF = 15
GLA_WIDTH = 512
GLA_HEADS = 4
GLA_HEAD_V = 128
GLA_HEAD_K = 64
GLA_KEY = 256
GLA_GATE_RANK = 16
GLA_GATE_NORM = 16.0
GLA_CHUNK = 64
D_FF = 2816
N_ADA = 9
EPS = 1e-6
PAST_LEN = 16384

LANES = 128
FFN_CHUNK = 256
TOKEN_TILE = 1024
FFN_TILE = 1024
FFN_SUB = 512
MIX_SUB = 512
POOL_HEAD = 32
SAMPLE_SEG = 24
VMEM_LIMIT = 56 * 1024 * 1024

BF16 = jnp.bfloat16
F32 = jnp.float32

_CAT_U, _CAT_Q, _CAT_K, _CAT_V, _CAT_OG, _CAT_GLR = 0, 512, 768, 1024, 1536, 2048
_CAT_WIDTH = 2176


def _dot(a, b):
    return jnp.dot(a, b, preferred_element_type=F32)


def _const_spec(shape):
    nd = len(shape)
    return pl.BlockSpec(shape, lambda *_: (0,) * nd, pipeline_mode=pl.Buffered(1))


def _rms(x, g):
    return x * lax.rsqrt(jnp.mean(x * x, axis=-1, keepdims=True) + EPS) * g


def _log_sigmoid(x):
    return jnp.minimum(x, 0.0) - jnp.log1p(jnp.exp(-jnp.abs(x)))


def _modulation(ada_ref, mod_ref, rows, rows_per_seq, seq=None):
    D = D_MODEL
    if rows_per_seq >= rows:
        ada = ada_ref[pl.ds(seq, 1), :]
        return lambda r: (ada[:, :D], ada[:, D:2 * D], ada[:, 2 * D:])
    for s in range(rows // rows_per_seq):
        mod_ref[s * rows_per_seq:(s + 1) * rows_per_seq, :] = jnp.broadcast_to(
            ada_ref[s:s + 1, :], (rows_per_seq, 3 * D))
    return lambda r: (mod_ref[r, :D], mod_ref[r, D:2 * D], mod_ref[r, 2 * D:])


def _ada_body(c_ref, w_ref, b_ref, o_ref):
    c = c_ref[...]
    a = (c * jax.nn.sigmoid(c)).astype(BF16)
    o_ref[...] = _dot(a, w_ref[...].astype(BF16)) + b_ref[...]


def _ada_call(c_all, w_ada, b_ada):
    n, d = c_all.shape
    nout = w_ada.shape[1]
    tn = 1024
    return pl.pallas_call(
        _ada_body,
        out_shape=jax.ShapeDtypeStruct((n, nout), F32),
        grid=(nout // tn,),
        in_specs=[pl.BlockSpec((n, d), lambda j: (0, 0)),
                  pl.BlockSpec((d, tn), lambda j: (0, j)),
                  pl.BlockSpec((1, tn), lambda j: (0, j))],
        out_specs=pl.BlockSpec((n, tn), lambda j: (0, j)),
        compiler_params=pltpu.CompilerParams(vmem_limit_bytes=VMEM_LIMIT),
        name="ada_proj",
    )(c_all, w_ada, b_ada.reshape(1, nout))


def _cast_in_weights(src_ref, dst_ref):
    src = src_ref[...]
    dst_ref[:, _CAT_U:_CAT_OG] = src[:, 0:1536].astype(BF16)
    dst_ref[:, _CAT_OG:_CAT_GLR] = src[:, 1552:2064].astype(BF16)
    glr = src[:, 1536:1552].astype(BF16)
    dst_ref[:, _CAT_GLR:_CAT_WIDTH] = jnp.concatenate(
        [glr, jnp.zeros((src.shape[0], LANES - GLA_GATE_RANK), BF16)], axis=1)


def _ffn_body(*refs, rows_per_seq, final, conv_kinds):
    n_conv = len(conv_kinds)
    x_ref, ada_ref, ln_ref, wup_ref, wdn_ref = refs[:5]
    rest = refs[5:]
    if final:
        lnf_ref, rest = rest[0], rest[1:]
    conv_in, rest = rest[:n_conv], rest[n_conv:]
    o_ref, rest = rest[0], rest[1:]
    conv_out, rest = rest[:n_conv], rest[n_conv:]
    h_ref, a_ref = rest[:2]
    mod_ref = rest[2] if len(rest) > 2 else None
    rows = x_ref.shape[0]
    seq = pl.program_id(0) // max(rows_per_seq // rows, 1)
    mod = _modulation(ada_ref, mod_ref, rows, rows_per_seq, seq)
    subs = [slice(lo, lo + FFN_SUB) for lo in range(0, rows, FFN_SUB)]

    def norm_rows(r):
        sh, sc, _ = mod(r)
        h_ref[r, :] = (_rms(x_ref[r, :], ln_ref[...]) * (1.0 + sc) + sh).astype(BF16)

    n_chunks = D_FF // FFN_CHUNK
    piece = FFN_SUB // 8
    norm_rows(subs[0])
    for si, r in enumerate(subs):
        for c in range(n_chunks):
            lo = c * FFN_CHUNK
            g = _dot(h_ref[r, :], wup_ref[:, lo:lo + FFN_CHUNK])
            u = _dot(h_ref[r, :], wup_ref[:, D_FF + lo:D_FF + lo + FFN_CHUNK])
            a_ref[r, lo:lo + FFN_CHUNK] = (g * jax.nn.sigmoid(g) * u).astype(BF16)
            if si + 1 < len(subs) and c < 8:
                nxt = subs[si + 1].start + c * piece
                norm_rows(slice(nxt, nxt + piece))
        y = x_ref[r, :] + 0.5 * mod(r)[2] * _dot(a_ref[r, :], wdn_ref[...])
        if final:
            y = _rms(y, lnf_ref[...])
        o_ref[r, :] = y
    for kind, src, dst in zip(conv_kinds, conv_in, conv_out):
        if kind == "w_in":
            _cast_in_weights(src, dst)
        else:
            dst[...] = src[...].astype(BF16)


def _ffn_call(x2d, ada, sub, ln, w_up, w_down, ln_final, *, rows_per_seq, name, ada_row0=0, convert=()):
    n, d = x2d.shape
    tm = min(FFN_TILE, n)
    steps = n // tm
    final = ln_final is not None
    if rows_per_seq >= tm:
        n_seq = n // rows_per_seq
        assert ada_row0 % n_seq == 0
        ada_spec = pl.BlockSpec((n_seq, 3 * d), lambda i: (ada_row0 // n_seq, sub))
        scratch_mod = []
    else:
        g = tm // rows_per_seq
        ada_spec = pl.BlockSpec((g, 3 * d), lambda i: (i, sub))
        scratch_mod = [pltpu.VMEM((tm, 3 * d), F32)]
    in_specs = [pl.BlockSpec((tm, d), lambda i: (i, 0)), ada_spec, _const_spec((1, d)),
                _const_spec(w_up.shape), _const_spec(w_down.shape)]
    args = [x2d, ada, ln.reshape(1, d), w_up, w_down]
    if final:
        in_specs.append(_const_spec((1, d)))
        args.append(ln_final.reshape(1, d))
    out_shape = [jax.ShapeDtypeStruct((n, d), F32)]
    out_specs = [pl.BlockSpec((tm, d), lambda i: (i, 0))]
    for kind, w in convert:
        slab = w.shape[0] // steps
        assert slab * steps == w.shape[0] and slab % 16 == 0, (w.shape, steps)
        cols = _CAT_WIDTH if kind == "w_in" else w.shape[1]
        in_specs.append(pl.BlockSpec((slab, w.shape[1]), lambda i: (i, 0)))
        args.append(w)
        out_shape.append(jax.ShapeDtypeStruct((w.shape[0], cols), BF16))
        out_specs.append(pl.BlockSpec((slab, cols), lambda i: (i, 0)))
    outs = pl.pallas_call(
        functools.partial(_ffn_body, rows_per_seq=rows_per_seq, final=final,
                          conv_kinds=tuple(kind for kind, _ in convert)),
        out_shape=tuple(out_shape),
        grid=(steps,),
        in_specs=in_specs,
        out_specs=tuple(out_specs),
        scratch_shapes=[pltpu.VMEM((tm, d), BF16), pltpu.VMEM((tm, D_FF), BF16)] + scratch_mod,
        compiler_params=pltpu.CompilerParams(dimension_semantics=("arbitrary",),
                                             vmem_limit_bytes=VMEM_LIMIT),
        name=name,
    )(*args)
    return outs[0], outs[1:]


def _project_in(h_ref, r, wcat_ref, wgk_ref, bgk_ref, u_dst, q_ref, k_ref, v_ref, og_ref, between=None):
    between = between or (lambda k: None)
    h = h_ref[r, :]
    glr = _dot(h, wcat_ref[:, _CAT_GLR:_CAT_WIDTH])
    q_ref[r, :] = _dot(h, wcat_ref[:, _CAT_Q:_CAT_K])
    between(0), between(1)
    k_ref[r, :] = _dot(h, wcat_ref[:, _CAT_K:_CAT_V])
    between(2), between(3)
    gk = _dot(glr.astype(BF16), wgk_ref[...]) + bgk_ref[...]
    v_ref[r, :] = _dot(h, wcat_ref[:, _CAT_V:_CAT_OG])
    between(4), between(5)
    u_dst[...] = _dot(h, wcat_ref[:, _CAT_U:_CAT_Q])
    between(6), between(7)
    og_ref[r, :] = _dot(h, wcat_ref[:, _CAT_OG:_CAT_GLR])
    return _log_sigmoid(gk) * (1.0 / GLA_GATE_NORM)


def _window_sums(ext_ref, s1_ref, s2_ref, s4_ref, s8_ref):
    n = ext_ref.shape[0]
    g = POOL_GROUP
    s1_ref[8:n, :] = ext_ref[8:n, :] + ext_ref[7:n - 1, :]
    s2_ref[16:n, :] = s1_ref[16:n, g:4 * g] + s1_ref[14:n - 2, g:4 * g]
    s4_ref[24:n, :] = s2_ref[24:n, g:3 * g] + s2_ref[20:n - 4, g:3 * g]
    s8_ref[32:n, :] = s4_ref[32:n, g:2 * g] + s4_ref[24:n - 8, g:2 * g]


def _head_stack(q_in, head_mask):
    return jnp.where(head_mask, jnp.concatenate([q_in] * GLA_HEADS, axis=0), 0.0).astype(BF16)


def _head_blocks(full, rows):
    return jnp.concatenate(
        [full[h * rows:(h + 1) * rows, h * GLA_HEAD_V:(h + 1) * GLA_HEAD_V] for h in range(GLA_HEADS)], axis=0)


def _gla_out(o_h, og_h, gnorm):
    o_n = o_h * lax.rsqrt(jnp.mean(o_h * o_h, axis=-1, keepdims=True) + EPS) * gnorm
    return (o_n * (og_h * jax.nn.sigmoid(og_h))).astype(BF16)


def _mix_prompt_body(x_ref, ada_ref, ln_ref, wcat_ref, wgk_ref, bgk_ref, wpool_ref, pscale_ref,
                     gnorm_ref, wout_ref, o_ref, pool_out_ref, gla_out_ref,
                     h_ref, ext_ref, s1_ref, s2_ref, s4_ref, s8_ref, q_ref, k_ref, v_ref, og_ref,
                     att_ref, zo_ref, state_ref, bl_ref, qin_ref, kin_ref, kdec_ref, v16_ref, upd_ref):
    j = pl.program_id(1)
    tt = x_ref.shape[0]
    n = POOL_HEAD + tt
    C = GLA_CHUNK

    @pl.when(j == 0)
    def _():
        ext_ref[0:POOL_HEAD, :] = jnp.zeros((POOL_HEAD, POOL_WIDTH), F32)
        state_ref[...] = jnp.zeros(state_ref.shape, F32)

    ada = ada_ref[pl.ds(pl.program_id(0), 1), :]
    sh, sc, gt = ada[:, :D_MODEL], ada[:, D_MODEL:2 * D_MODEL], ada[:, 2 * D_MODEL:]

    def norm_rows(r):
        h_ref[r, :] = (_rms(x_ref[r, :], ln_ref[...]) * (1.0 + sc) + sh).astype(BF16)

    subs = [slice(lo, lo + MIX_SUB) for lo in range(0, tt, MIX_SUB)]
    piece = MIX_SUB // 8
    norm_rows(subs[0])
    la_parts = []
    for si, r in enumerate(subs):
        def between(k, si=si):
            if si + 1 < len(subs):
                lo = subs[si + 1].start + k * piece
                norm_rows(slice(lo, lo + piece))
        la_parts.append(_project_in(h_ref, r, wcat_ref, wgk_ref, bgk_ref,
                                    ext_ref.at[POOL_HEAD + r.start:POOL_HEAD + r.stop, :],
                                    q_ref, k_ref, v_ref, og_ref, between))
    la = jnp.concatenate(la_parts, axis=0)

    nc = tt // C
    step = lax.broadcasted_iota(jnp.int32, (tt, GLA_KEY), 0) % C
    b = la
    shift = 1
    while shift < C:
        b = b + jnp.where(step >= shift, pltpu.roll(b, shift, axis=0), 0.0)
        shift *= 2
    for c in range(nc):
        bl_ref[c:c + 1, :] = b[(c + 1) * C - 1:(c + 1) * C, :]
    b_last = jnp.broadcast_to(bl_ref[...][:, None, :], (nc, C, GLA_KEY)).reshape(tt, GLA_KEY)
    kk = k_ref[...]
    qin_ref[...] = (q_ref[...] * jnp.exp(b) * (GLA_HEAD_K ** -0.5)).astype(BF16)
    kin_ref[...] = (kk * jnp.exp(-b)).astype(BF16)
    kdec_ref[...] = (kk * jnp.exp(b_last - b)).astype(BF16)
    v16_ref[...] = v_ref[...].astype(BF16)
    decay_t = jnp.exp(bl_ref[...]).T

    _window_sums(ext_ref, s1_ref, s2_ref, s4_ref, s8_ref)
    pos1 = lax.broadcasted_iota(jnp.int32, (tt, POOL_GROUP), 0) + (j * tt + 1)
    wins = (s1_ref, s2_ref, s4_ref, s8_ref)
    for g, w in enumerate(POOL_WINDOWS):
        lanes = slice(g * POOL_GROUP, (g + 1) * POOL_GROUP)
        cnt = jnp.minimum(pos1, w).astype(F32)
        p = wins[g][POOL_HEAD:n, 0:POOL_GROUP] / cnt - ext_ref[POOL_HEAD:n, lanes]
        z = _dot(p.astype(BF16), wpool_ref[g]) * pscale_ref[:, lanes]
        zo_ref[:, lanes] = z.astype(BF16)

    row = lax.broadcasted_iota(jnp.int32, (GLA_HEADS * C, GLA_KEY), 0)
    lane = lax.broadcasted_iota(jnp.int32, (GLA_HEADS * C, GLA_KEY), 1)
    head_mask = (row // C) == (lane // GLA_HEAD_K)
    arow = lax.broadcasted_iota(jnp.int32, (GLA_HEADS * C, C), 0)
    acol = lax.broadcasted_iota(jnp.int32, (GLA_HEADS * C, C), 1)
    causal = (arow % C) >= acol
    state = state_ref[...]
    gnorm = gnorm_ref[...]
    per_sub = MIX_SUB // C

    def project_out(rs):
        o_ref[rs, :] = x_ref[rs, :] + gt * _dot(zo_ref[rs, :], wout_ref[...])

    for si, rs in enumerate(subs):
        chunks = range(si * per_sub, (si + 1) * per_sub)
        for c in chunks:
            r = slice(c * C, (c + 1) * C)
            qm = jnp.where(head_mask, jnp.concatenate([qin_ref[r, :]] * GLA_HEADS, axis=0), 0.0)
            att = lax.dot_general(qm, kin_ref[r, :], (((1,), (1,)), ((), ())), preferred_element_type=F32)
            att_ref[c] = jnp.where(causal, att, 0.0).astype(BF16)
        for c in chunks:
            r = slice(c * C, (c + 1) * C)
            blocks = []
            for p in range(GLA_HEADS // 2):
                full = lax.dot_general(kdec_ref[r, 2 * p * GLA_HEAD_K:2 * (p + 1) * GLA_HEAD_K],
                                       v16_ref[r, 2 * p * GLA_HEAD_V:2 * (p + 1) * GLA_HEAD_V],
                                       (((0,), (0,)), ((), ())), preferred_element_type=F32)
                blocks += [full[0:GLA_HEAD_K, 0:GLA_HEAD_V], full[GLA_HEAD_K:, GLA_HEAD_V:]]
            upd_ref[c] = jnp.concatenate(blocks, axis=0)
        for c in chunks:
            r = slice(c * C, (c + 1) * C)
            s16 = state.astype(BF16)
            for h in range(GLA_HEADS):
                vl = slice(h * GLA_HEAD_V, (h + 1) * GLA_HEAD_V)
                kl = slice(h * GLA_HEAD_K, (h + 1) * GLA_HEAD_K)
                lhs = jnp.concatenate([att_ref[c, h * C:(h + 1) * C, :], qin_ref[r, kl]], axis=1)
                rhs = jnp.concatenate([v16_ref[r, vl], s16[kl, :]], axis=0)
                zo_ref[r, POOL_WIDTH + h * GLA_HEAD_V:POOL_WIDTH + (h + 1) * GLA_HEAD_V] = _gla_out(
                    _dot(lhs, rhs), og_ref[r, vl], gnorm)
            state = decay_t[:, c:c + 1] * state + upd_ref[c]
        if si > 0:
            project_out(subs[si - 1])
    project_out(subs[-1])
    state_ref[...] = state

    @pl.when(j == pl.num_programs(1) - 1)
    def _():
        pool_out_ref[0] = ext_ref[n - POOL_BUF:n, :]
        gla_out_ref[0] = state_ref[...]

    ext_ref[0:POOL_HEAD, :] = ext_ref[tt:n, :]


def _mix_prompt_call(x2d, ada, ln, wcat, wgk, bgk, wpool, pscale, gnorm, wout, *, batch, seq, ada_row0):
    d = D_MODEL
    tt = TOKEN_TILE
    nt = seq // tt
    n = POOL_HEAD + tt
    return pl.pallas_call(
        _mix_prompt_body,
        out_shape=(jax.ShapeDtypeStruct((batch * seq, d), F32),
                   jax.ShapeDtypeStruct((batch, POOL_BUF, POOL_WIDTH), F32),
                   jax.ShapeDtypeStruct((batch, GLA_KEY, GLA_HEAD_V), F32)),
        grid=(batch, nt),
        in_specs=[pl.BlockSpec((tt, d), lambda b, j: (b * nt + j, 0)),
                  pl.BlockSpec((batch, 3 * d), lambda b, j: (ada_row0 // batch, 1)),
                  _const_spec((1, d)), _const_spec(wcat.shape), _const_spec(wgk.shape),
                  _const_spec((1, GLA_KEY)), _const_spec(wpool.shape), _const_spec((1, POOL_WIDTH)),
                  _const_spec((1, GLA_HEAD_V)), _const_spec(wout.shape)],
        out_specs=(pl.BlockSpec((tt, d), lambda b, j: (b * nt + j, 0)),
                   pl.BlockSpec((1, POOL_BUF, POOL_WIDTH), lambda b, j: (b, 0, 0)),
                   pl.BlockSpec((1, GLA_KEY, GLA_HEAD_V), lambda b, j: (b, 0, 0))),
        scratch_shapes=[pltpu.VMEM((tt, d), BF16),
                        pltpu.VMEM((n, POOL_WIDTH), F32), pltpu.VMEM((n, 4 * POOL_GROUP), F32),
                        pltpu.VMEM((n, 3 * POOL_GROUP), F32), pltpu.VMEM((n, 2 * POOL_GROUP), F32),
                        pltpu.VMEM((n, POOL_GROUP), F32),
                        pltpu.VMEM((tt, GLA_KEY), F32), pltpu.VMEM((tt, GLA_KEY), F32),
                        pltpu.VMEM((tt, GLA_WIDTH), F32), pltpu.VMEM((tt, GLA_WIDTH), F32),
                        pltpu.VMEM((tt // GLA_CHUNK, GLA_HEADS * GLA_CHUNK, GLA_CHUNK), BF16),
                        pltpu.VMEM((tt, d), BF16),
                        pltpu.VMEM((GLA_KEY, GLA_HEAD_V), F32),
                        pltpu.VMEM((tt // GLA_CHUNK, GLA_KEY), F32),
                        pltpu.VMEM((tt, GLA_KEY), BF16), pltpu.VMEM((tt, GLA_KEY), BF16),
                        pltpu.VMEM((tt, GLA_KEY), BF16), pltpu.VMEM((tt, GLA_WIDTH), BF16),
                        pltpu.VMEM((tt // GLA_CHUNK, GLA_KEY, GLA_HEAD_V), F32)],
        compiler_params=pltpu.CompilerParams(dimension_semantics=("arbitrary", "arbitrary"),
                                             vmem_limit_bytes=VMEM_LIMIT),
        name="mix_prompt",
    )(x2d, ada, ln.reshape(1, d), wcat, wgk, bgk.reshape(1, GLA_KEY), wpool,
      pscale.reshape(1, POOL_WIDTH), gnorm.reshape(1, GLA_HEAD_V), wout)


def _mix_sample_body(x_ref, ada_ref, pool_ref, gla_ref, ln_ref, wcat_ref, wgk_ref, bgk_ref, wpool_ref,
                     pscale_ref, gnorm_ref, wout_ref, o_ref, pool_out_ref, gla_out_ref,
                     mod_ref, h_ref, u_ref, ext_ref, s1_ref, s2_ref, s4_ref, s8_ref, p_ref,
                     q_ref, k_ref, v_ref, og_ref, zo_ref, oint_ref, *, steps, pos0):
    rows = x_ref.shape[0]
    T = steps
    G = rows // T
    C = GLA_CHUNK
    SEG = SAMPLE_SEG

    sh, sc, gt = _modulation(ada_ref, mod_ref, rows, T)(slice(None))
    h_ref[...] = (_rms(x_ref[...], ln_ref[...]) * (1.0 + sc) + sh).astype(BF16)
    la = _project_in(h_ref, slice(0, rows), wcat_ref, wgk_ref, bgk_ref, u_ref, q_ref, k_ref, v_ref, og_ref)

    ext_ref[0:POOL_HEAD, :] = jnp.zeros((POOL_HEAD, POOL_WIDTH), F32)
    for s in range(G):
        base = POOL_HEAD + s * SEG
        ext_ref[base:base + 16, :] = pool_ref[s]
        ext_ref[base + 16:base + SEG, :] = u_ref[s * T:(s + 1) * T, :]
    _window_sums(ext_ref, s1_ref, s2_ref, s4_ref, s8_ref)
    pos1 = lax.broadcasted_iota(jnp.int32, (T, POOL_GROUP), 0) + (pos0 + 1)
    wins = (s1_ref, s2_ref, s4_ref, s8_ref)
    for s in range(G):
        base = POOL_HEAD + s * SEG
        for g, w in enumerate(POOL_WINDOWS):
            lanes = slice(g * POOL_GROUP, (g + 1) * POOL_GROUP)
            cnt = jnp.minimum(pos1, w).astype(F32)
            p_ref[s * T:(s + 1) * T, lanes] = (wins[g][base + 16:base + SEG, 0:POOL_GROUP] / cnt
                                               - u_ref[s * T:(s + 1) * T, lanes])
        pool_out_ref[s] = ext_ref[base + SEG - POOL_BUF:base + SEG, :]
    for g in range(len(POOL_WINDOWS)):
        lanes = slice(g * POOL_GROUP, (g + 1) * POOL_GROUP)
        z = _dot(p_ref[:, lanes].astype(BF16), wpool_ref[g]) * pscale_ref[:, lanes]
        zo_ref[:, lanes] = z.astype(BF16)

    step = lax.broadcasted_iota(jnp.int32, (rows, GLA_KEY), 0) % T
    b = la
    shift = 1
    while shift < T:
        b = b + jnp.where(step >= shift, pltpu.roll(b, shift, axis=0), 0.0)
        shift *= 2
    b_last = jnp.broadcast_to(b.reshape(G, T, GLA_KEY)[:, T - 1:T, :], (G, T, GLA_KEY)).reshape(rows, GLA_KEY)
    kk = k_ref[...]
    q_in = q_ref[...] * jnp.exp(b) * (GLA_HEAD_K ** -0.5)
    k_in = (kk * jnp.exp(-b)).astype(BF16)
    k_dec = (kk * jnp.exp(b_last - b)).astype(BF16)
    v_all = v_ref[...].astype(BF16)
    decay_t = jnp.exp(b_last).T

    hrow = lax.broadcasted_iota(jnp.int32, (GLA_HEADS * T, GLA_KEY), 0)
    hlane = lax.broadcasted_iota(jnp.int32, (GLA_HEADS * T, GLA_KEY), 1)
    seq_head_mask = (hrow // T) == (hlane // GLA_HEAD_K)
    for s in range(G):
        r = slice(s * T, (s + 1) * T)
        state = gla_ref[s]
        qm = _head_stack(q_in[r, :], seq_head_mask)
        o_int = _dot(qm, state.astype(BF16))
        for h in range(GLA_HEADS):
            oint_ref[h, r, :] = o_int[h * T:(h + 1) * T, :]
        upd = _head_blocks(lax.dot_general(k_dec[r, :], v_all[r, :], (((0,), (0,)), ((), ())),
                                           preferred_element_type=F32), GLA_HEAD_K)
        gla_out_ref[s] = decay_t[:, s * T:s * T + 1] * state + upd

    row = lax.broadcasted_iota(jnp.int32, (GLA_HEADS * C, GLA_KEY), 0)
    lane = lax.broadcasted_iota(jnp.int32, (GLA_HEADS * C, GLA_KEY), 1)
    head_mask = (row // C) == (lane // GLA_HEAD_K)
    arow = lax.broadcasted_iota(jnp.int32, (GLA_HEADS * C, C), 0) % C
    acol = lax.broadcasted_iota(jnp.int32, (GLA_HEADS * C, C), 1)
    causal = (arow >= acol) & ((arow // T) == (acol // T))
    gnorm = gnorm_ref[...]
    for c in range(rows // C):
        r = slice(c * C, (c + 1) * C)
        qm = _head_stack(q_in[r, :], head_mask)
        att = lax.dot_general(qm, k_in[r, :], (((1,), (1,)), ((), ())), preferred_element_type=F32)
        att = jnp.where(causal, att, 0.0).astype(BF16)
        for h in range(GLA_HEADS):
            vl = slice(h * GLA_HEAD_V, (h + 1) * GLA_HEAD_V)
            o_h = _dot(att[h * C:(h + 1) * C, :], v_all[r, vl]) + oint_ref[h, r, :]
            zo_ref[r, POOL_WIDTH + h * GLA_HEAD_V:POOL_WIDTH + (h + 1) * GLA_HEAD_V] = _gla_out(
                o_h, og_ref[r, vl], gnorm)

    o_ref[...] = x_ref[...] + gt * _dot(zo_ref[...], wout_ref[...])


def _mix_sample_call(x2d, ada, pool16, gla_state, ln, wcat, wgk, bgk, wpool, pscale, gnorm, wout, *,
                     steps, pos0):
    d = D_MODEL
    nseq = gla_state.shape[0]
    g = 32
    rows = g * steps
    n = POOL_HEAD + g * SAMPLE_SEG
    return pl.pallas_call(
        functools.partial(_mix_sample_body, steps=steps, pos0=pos0),
        out_shape=(jax.ShapeDtypeStruct((nseq * steps, d), F32),
                   jax.ShapeDtypeStruct((nseq, POOL_BUF, POOL_WIDTH), F32),
                   jax.ShapeDtypeStruct((nseq, GLA_KEY, GLA_HEAD_V), F32)),
        grid=(nseq // g,),
        in_specs=[pl.BlockSpec((rows, d), lambda i: (i, 0)),
                  pl.BlockSpec((g, 3 * d), lambda i: (i, 1)),
                  pl.BlockSpec((g, POOL_BUF + 1, POOL_WIDTH), lambda i: (i, 0, 0)),
                  pl.BlockSpec((g, GLA_KEY, GLA_HEAD_V), lambda i: (i, 0, 0)),
                  _const_spec((1, d)), _const_spec(wcat.shape), _const_spec(wgk.shape),
                  _const_spec((1, GLA_KEY)), _const_spec(wpool.shape), _const_spec((1, POOL_WIDTH)),
                  _const_spec((1, GLA_HEAD_V)), _const_spec(wout.shape)],
        out_specs=(pl.BlockSpec((rows, d), lambda i: (i, 0)),
                   pl.BlockSpec((g, POOL_BUF, POOL_WIDTH), lambda i: (i, 0, 0)),
                   pl.BlockSpec((g, GLA_KEY, GLA_HEAD_V), lambda i: (i, 0, 0))),
        scratch_shapes=[pltpu.VMEM((rows, 3 * d), F32), pltpu.VMEM((rows, d), BF16),
                        pltpu.VMEM((rows, POOL_WIDTH), F32),
                        pltpu.VMEM((n, POOL_WIDTH), F32), pltpu.VMEM((n, 4 * POOL_GROUP), F32),
                        pltpu.VMEM((n, 3 * POOL_GROUP), F32), pltpu.VMEM((n, 2 * POOL_GROUP), F32),
                        pltpu.VMEM((n, POOL_GROUP), F32), pltpu.VMEM((rows, POOL_WIDTH), F32),
                        pltpu.VMEM((rows, GLA_KEY), F32), pltpu.VMEM((rows, GLA_KEY), F32),
                        pltpu.VMEM((rows, GLA_WIDTH), F32), pltpu.VMEM((rows, GLA_WIDTH), F32),
                        pltpu.VMEM((rows, d), BF16),
                        pltpu.VMEM((GLA_HEADS, rows, GLA_HEAD_V), F32)],
        compiler_params=pltpu.CompilerParams(dimension_semantics=("arbitrary",),
                                             vmem_limit_bytes=VMEM_LIMIT),
        name="mix_sample",
    )(x2d, ada, pool16, gla_state, ln.reshape(1, d), wcat, wgk, bgk.reshape(1, GLA_KEY), wpool,
      pscale.reshape(1, POOL_WIDTH), gnorm.reshape(1, GLA_HEAD_V), wout)


def kernel(x_prompt, x_sample, state_pool, state_gla, c_prompt, c_sample, ln_ffn1, ln_mix, ln_ffn2,
           w_ada, b_ada, w_ffn1_up, w_ffn1_down, w_in, w_gk2, b_gk2, w_pool, pool_scale, gla_norm,
           w_out, w_ffn2_up, w_ffn2_down, ln_final):
    bp, seq, d = x_prompt.shape
    bs, steps, _ = x_sample.shape
    depth = ln_ffn1.shape[0]
    xp = x_prompt.reshape(bp * seq, d)
    xs = x_sample.reshape(bs * steps, d)
    c_all = jnp.concatenate([c_sample, c_prompt], axis=0)
    pool_p, gla_p, pool_s, gla_s = [], [], [], []
    for l in range(depth):
        ada = _ada_call(c_all, w_ada[l], b_ada[l])
        w1u, w1d = w_ffn1_up[l].astype(BF16), w_ffn1_down[l].astype(BF16)
        wgk = jnp.pad(w_gk2[l], ((0, LANES - GLA_GATE_RANK), (0, 0))).astype(BF16)
        lnf = ln_final if l == depth - 1 else None

        xp, (w2u, w2d, wout, wcat, wpool) = _ffn_call(
            xp, ada, 0, ln_ffn1[l], w1u, w1d, None, rows_per_seq=seq, name="ffn1_prompt", ada_row0=bs,
            convert=(("plain", w_ffn2_up[l]), ("plain", w_ffn2_down[l]), ("plain", w_out[l]), ("w_in", w_in[l]),
                     ("plain", w_pool[l].reshape(N_POOL_GROUPS * POOL_GROUP, POOL_GROUP))))
        wpool = wpool.reshape(N_POOL_GROUPS, POOL_GROUP, POOL_GROUP)
        xs, _ = _ffn_call(xs, ada, 0, ln_ffn1[l], w1u, w1d, None, rows_per_seq=steps, name="ffn1_sample")

        xp, nb_p, ns_p = _mix_prompt_call(xp, ada, ln_mix[l], wcat, wgk, b_gk2[l], wpool, pool_scale[l],
                                          gla_norm[l], wout, batch=bp, seq=seq, ada_row0=bs)
        pool16 = jnp.pad(state_pool[l], ((0, 0), (1, 0), (0, 0)))
        xs, nb_s, ns_s = _mix_sample_call(xs, ada, pool16, state_gla[l].reshape(bs, GLA_KEY, GLA_HEAD_V),
                                          ln_mix[l], wcat, wgk, b_gk2[l], wpool, pool_scale[l], gla_norm[l],
                                          wout, steps=steps, pos0=PAST_LEN)

        xp, _ = _ffn_call(xp, ada, 2, ln_ffn2[l], w2u, w2d, lnf, rows_per_seq=seq, name="ffn2_prompt",
                          ada_row0=bs)
        xs, _ = _ffn_call(xs, ada, 2, ln_ffn2[l], w2u, w2d, lnf, rows_per_seq=steps, name="ffn2_sample")
        pool_p.append(nb_p)
        gla_p.append(ns_p.reshape(bp, GLA_HEADS, GLA_HEAD_K, GLA_HEAD_V))
        pool_s.append(nb_s)
        gla_s.append(ns_s.reshape(bs, GLA_HEADS, GLA_HEAD_K, GLA_HEAD_V))
    return (xp.reshape(bp, seq, d), xs.reshape(bs, steps, d), jnp.stack(pool_p), jnp.stack(gla_p),
            jnp.stack(pool_s), jnp.stack(gla_s))
```
